```python
import jax, jax.numpy as jnp
from jax import lax
import numpy as np

D_MODEL = 1024
BATCH = 32
SEQ = 256
DEPTH = 1
DEC_BATCH = 8
DEC_SEQ = 4096
PAST_LEN = 256

GRID_W = 64
MIX_WIDTH = D_MODEL
A_WIDTH = MIX_WIDTH // 2
B_WIDTH = MIX_WIDTH - A_WIDTH
HEAD_DIM = 64
A_HEADS = A_WIDTH // HEAD_DIM
DECAY_LORA = 64
ICLR_LORA = 64
IN_WIDTH = 4 * A_WIDTH + 4 * B_WIDTH
IN_SPLITS = tuple(int(s) for s in np.cumsum([A_WIDTH] * 4 + [B_WIDTH] * 3))
NORM_EPS = 1e-6
GN_EPS = 64e-5

kernel_name = "bidir_rwkv7_shortconv_hybrid_dit_step"


def _rmsnorm(x, g):
    xf = x.astype(jnp.float32)
    y = xf * lax.rsqrt(jnp.mean(xf * xf, axis=-1, keepdims=True) + NORM_EPS)
    return (y * g.astype(jnp.float32)).astype(x.dtype)


def _centred_taps(p, axis):
    n = p.shape[axis]
    pad = [(0, 0)] * p.ndim
    pad[axis] = (1, 1)
    pp = jnp.pad(p, pad)
    return (lax.slice_in_dim(pp, 0, n, axis=axis), lax.slice_in_dim(pp, 2, n + 2, axis=axis))


def _token_shift(p, mu, is_latent):
    b, t, ch = p.shape
    if is_latent:
        q = p.reshape(b, t // GRID_W, GRID_W, ch)
        axis = 2
    else:
        q = p
        axis = 1
    prev, nxt = _centred_taps(q, axis)
    return (q + mu * (0.5 * (prev + nxt) - q)).reshape(b, t, ch)


def _short_conv(u, w, is_latent):
    if not is_latent:
        prev, nxt = _centred_taps(u, 1)
        return w[0] * prev + w[1] * u + w[2] * nxt
    b, t, ch = u.shape
    half = ch // 2
    g = u.reshape(b, t // GRID_W, GRID_W, ch)
    ph, nh = _centred_taps(g[..., :half], 2)
    pv, nv = _centred_taps(g[..., half:], 1)
    prev = jnp.concatenate([ph, pv], axis=-1)
    nxt = jnp.concatenate([nh, nv], axis=-1)
    return (w[0] * prev + w[1] * g + w[2] * nxt).reshape(b, t, ch)


def _heads(z):
    return z.reshape(z.shape[:-1] + (A_HEADS, HEAD_DIM))


def _wkv(h, r, k, v, s0, lp):
    f32 = jnp.float32
    b, t, _ = h.shape
    hf = h.astype(f32)
    lw = jnp.tanh(jnp.einsum('btd,edl->betl', hf, lp['decay_down'].astype(f32)))
    w = lp['decay_w0'].astype(f32)[None, :, None, :] + jnp.einsum('betl,elc->betc', lw, lp['decay_up'].astype(f32))
    decay = jnp.exp(-jnp.exp(-jax.nn.softplus(-w) - 0.5))
    la = jnp.einsum('btd,edl->betl', hf, lp['iclr_down'].astype(f32))
    a = jax.nn.sigmoid(lp['iclr_bias'].astype(f32)[None, :, None, :]
                       + jnp.einsum('betl,elc->betc', la, lp['iclr_up'].astype(f32)))
    rf, kf, vf = r.astype(f32), k.astype(f32), v.astype(f32)
    kk = _heads(kf * lp['kk_scale'].astype(f32))
    kk = kk * lax.rsqrt(jnp.maximum(jnp.sum(kk * kk, axis=-1, keepdims=True), 1e-24))
    kd = kf[:, None] * (1.0 + (a - 1.0) * lp['ka_scale'].astype(f32))
    rh, vh, kdh, ah, dh = _heads(rf), _heads(vf), _heads(kd), _heads(a), _heads(decay)

    def both(z):
        return jnp.stack([z, z], axis=1)

    def orient(z):
        return jnp.moveaxis(jnp.stack([z[:, 0], jnp.flip(z[:, 1], axis=1)], axis=1), 2, 0)

    xs = (orient(both(rh)), orient(dh), orient(kdh), orient(both(vh)), orient(both(kk)), orient(ah))

    def step(S, inp):
        r_t, w_t, k_t, v_t, kk_t, a_t = inp
        s_kk = jnp.einsum('bdhij,bdhj->bdhi', S, kk_t)
        S = (S * w_t[..., None, :] - s_kk[..., :, None] * (kk_t * a_t)[..., None, :]
             + v_t[..., :, None] * k_t[..., None, :])
        return S, jnp.einsum('bdhij,bdhj->bdhi', S, r_t)

    s_final, ys = lax.scan(step, s0.astype(f32), xs)
    ys = jnp.moveaxis(ys, 0, 2)
    y = ys[:, 0] + jnp.flip(ys[:, 1], axis=1)
    mean = jnp.mean(y, axis=-1, keepdims=True)
    var = jnp.mean(jnp.square(y - mean), axis=-1, keepdims=True)
    y = ((y - mean) * lax.rsqrt(var + GN_EPS)).reshape(b, t, A_WIDTH)
    y = y * lp['gn_w'].astype(f32) + lp['gn_b'].astype(f32)
    bonus = jnp.sum(rh[:, None] * kdh * _heads(lp['bonus_rk'].astype(f32)), axis=(1, -1))
    y = y + (bonus[..., None] * vh).reshape(b, t, A_WIDTH)
    return y, s_final


def _layer(x, mod, s0, is_latent, lp):
    shift, scale, gate = jnp.split(mod, 3, axis=-1)
    h = _rmsnorm(x, lp['norm_g']) * (1.0 + scale) + shift
    proj = h @ lp['w_in']
    r, k, v, g_a, b_gate, c_gate, u_conv, g_b = jnp.split(proj, IN_SPLITS, axis=-1)
    rkv = _token_shift(jnp.concatenate([r, k, v], axis=-1), lp['shift_mu'].reshape(-1), is_latent)
    r, k, v = jnp.split(rkv, 3, axis=-1)
    ya, s_final = _wkv(h, r, k, v, s0, lp)
    ya = ya.astype(x.dtype) * jax.nn.silu(g_a)
    yb = b_gate * _short_conv(c_gate * u_conv, lp['conv_w'], is_latent) * jax.nn.silu(g_b)
    u = jnp.concatenate([ya, yb], axis=-1) @ lp['w_out']
    return x + gate * u, s_final


def setup_inputs(seed: int = 0) -> dict:
    key = jax.random.key(seed)
    ks = jax.random.split(key, 32)
    f32 = jnp.float32
    L = DEPTH

    def nrm(k, shape, s):
        return jax.random.normal(k, shape, f32) * s

    return {
        "x_prompt": nrm(ks[0], (BATCH, SEQ, D_MODEL), 1.0),
        "x_sample": nrm(ks[1], (DEC_BATCH, DEC_SEQ, D_MODEL), 1.0),
        "c": nrm(ks[2], (DEC_BATCH, D_MODEL), 1.0),
        "state_wkv": nrm(ks[3], (DEC_BATCH, DEPTH, 2, A_HEADS, HEAD_DIM, HEAD_DIM), 0.5),
        "c_ctx": nrm(ks[4], (D_MODEL,), 1.0),
        "w_ada": nrm(ks[5], (L, D_MODEL, 3 * D_MODEL), 0.5 * D_MODEL ** -0.5),
        "b_ada": nrm(ks[6], (L, 3 * D_MODEL), 0.02),
        "norm_g": 1.0 + nrm(ks[7], (L, D_MODEL), 0.02),
        "w_in": nrm(ks[8], (L, D_MODEL, IN_WIDTH), D_MODEL ** -0.5),
        "shift_mu": jax.random.uniform(ks[9], (L, 3, A_WIDTH), f32),
        "decay_w0": jax.random.uniform(ks[10], (L, 2, A_WIDTH), f32, -4.0, 1.0),
        "decay_down": nrm(ks[11], (L, 2, D_MODEL, DECAY_LORA), D_MODEL ** -0.5),
        "decay_up": nrm(ks[12], (L, 2, DECAY_LORA, A_WIDTH), 0.5 * DECAY_LORA ** -0.5),
        "iclr_bias": nrm(ks[13], (L, 2, A_WIDTH), 0.5),
        "iclr_down": nrm(ks[14], (L, 2, D_MODEL, ICLR_LORA), D_MODEL ** -0.5),
        "iclr_up": nrm(ks[15], (L, 2, ICLR_LORA, A_WIDTH), 0.5 * ICLR_LORA ** -0.5),
        "kk_scale": 0.85 + nrm(ks[16], (L, A_WIDTH), 0.05),
        "ka_scale": 1.0 + nrm(ks[17], (L, A_WIDTH), 0.05),
        "bonus_rk": nrm(ks[18], (L, A_WIDTH), 0.1),
        "gn_w": 1.0 + nrm(ks[19], (L, A_WIDTH), 0.02),
        "gn_b": nrm(ks[20], (L, A_WIDTH), 0.02),
        "conv_w": nrm(ks[21], (L, 3, B_WIDTH), 3.0 ** -0.5),
        "w_out": nrm(ks[22], (L, MIX_WIDTH, D_MODEL), MIX_WIDTH ** -0.5),
        "final_g": 1.0 + nrm(ks[23], (D_MODEL,), 0.02),
    }


def reference(x_prompt, x_sample, c, state_wkv, c_ctx, w_ada, b_ada, norm_g, w_in, shift_mu,
              decay_w0, decay_down, decay_up, iclr_bias, iclr_down, iclr_up, kk_scale, ka_scale,
              bonus_rk, gn_w, gn_b, conv_w, w_out, final_g):
    ctx = x_prompt
    lat = x_sample
    ctx_states = []
    for l in range(DEPTH):
        lp = dict(norm_g=norm_g[l], w_in=w_in[l], shift_mu=shift_mu[l], decay_w0=decay_w0[l],
                  decay_down=decay_down[l], decay_up=decay_up[l], iclr_bias=iclr_bias[l],
                  iclr_down=iclr_down[l], iclr_up=iclr_up[l], kk_scale=kk_scale[l],
                  ka_scale=ka_scale[l], bonus_rk=bonus_rk[l], gn_w=gn_w[l], gn_b=gn_b[l],
                  conv_w=conv_w[l], w_out=w_out[l])
        mod_ctx = (jax.nn.silu(c_ctx) @ w_ada[l] + b_ada[l])[None, None, :]
        mod_lat = (jax.nn.silu(c) @ w_ada[l] + b_ada[l])[:, None, :]
        s_zero = jnp.zeros((ctx.shape[0], 2, A_HEADS, HEAD_DIM, HEAD_DIM), jnp.float32)
        ctx, s_ctx = _layer(ctx, mod_ctx, s_zero, False, lp)
        ctx_states.append(s_ctx.astype(x_prompt.dtype))
        lat, _ = _layer(lat, mod_lat, state_wkv[:, l], True, lp)
    y_prompt = _rmsnorm(ctx, final_g)
    y_sample = _rmsnorm(lat, final_g)
    new_state_wkv = jnp.stack(ctx_states, axis=1)
    return (y_prompt, y_sample, new_state_wkv)
```

```python
import functools
import math

import numpy as np
import jax
import jax.numpy as jnp
from jax import lax
from jax.experimental import pallas as pl
from jax.experimental.pallas import tpu as pltpu

D_MODEL = 1024
A_WIDTH = 512
B_WIDTH = 512
HEAD_DIM = 64
A_HEADS = 8
GRID_W = 64
LORA = 64
IN_WIDTH = 4 * A_WIDTH + 4 * B_WIDTH
NORM_EPS = 1e-6
GN_EPS = 64e-5

CHUNK = 64
GROUP = 256
HEADS_PER_GROUP = GROUP // HEAD_DIM
N_GROUPS = A_WIDTH // GROUP
TOKEN_TILE = 256
HALO = GRID_W
VMEM_LIMIT = 56 * 1024 * 1024

F32 = jnp.float32
BF16 = jnp.bfloat16


def _split2(x):
    hi = x.astype(BF16)
    lo = (x - hi.astype(F32)).astype(BF16)
    return hi, lo


def _dg(a, b, nt=False):
    dims = (((1,), (1 if nt else 0,)), ((), ()))
    return lax.dot_general(a, b, dims, preferred_element_type=F32)


def _segsum(z, ones_ref):
    outs = []
    for g in range(N_GROUPS):
        zh, zl = _split2(z[:, g * GROUP:(g + 1) * GROUP])
        outs.append(_dg(zh, ones_ref[...]) + _dg(zl, ones_ref[...]))
    return jnp.concatenate(outs, axis=1)


def _adaln_kernel(c_ref, w_ref, b_ref, o_ref):
    c = c_ref[...]
    s = c * jax.nn.sigmoid(c)
    o_ref[...] = _dg(s.astype(BF16), w_ref[...]) + b_ref[...]


def _adaln(cvec, w_ada_bf, b_ada):
    rows = cvec.shape[0]
    return pl.pallas_call(
        _adaln_kernel,
        out_shape=jax.ShapeDtypeStruct((rows, 3 * D_MODEL), F32),
        compiler_params=pltpu.CompilerParams(vmem_limit_bytes=VMEM_LIMIT),
        name="adaln",
    )(cvec, w_ada_bf, b_ada)


def _shift_rows(p, seg):
    rows = p.shape[0]
    assert seg & (seg - 1) == 0
    pos = lax.broadcasted_iota(jnp.int32, p.shape, 0) & (seg - 1)
    prev = jnp.where(pos == 0, 0.0, pltpu.roll(p, 1, 0))
    nxt = jnp.where(pos == seg - 1, 0.0, pltpu.roll(p, rows - 1, 0))
    return prev, nxt


def _pre_kernel(x_ref, mod_ref, g_ref, win_ref, wdn_ref, wdu_ref, wiu_ref, w0_ref, ab_ref, mu_ref,
                kks_ref, kas_ref, brk_ref, ones_ref,
                r_o, v_o, kk_o, lw0_o, lw1_o, kd0_o, kd1_o, b0_o, b1_o, bv_o, ga_o, cu_o, bg_o, *, seg):
    x = x_ref[0]
    mod = mod_ref[0]
    shift = mod[:, :D_MODEL]
    scale = mod[:, D_MODEL:2 * D_MODEL]
    xn = x * lax.rsqrt(jnp.mean(x * x, axis=-1, keepdims=True) + NORM_EPS) * g_ref[...]
    h = xn * (1.0 + scale) + shift
    hb = h.astype(BF16)

    def proj(i):
        return _dg(hb, win_ref[:, i * A_WIDTH:(i + 1) * A_WIDTH])

    def mixed(i):
        p = proj(i)
        prev, nxt = _shift_rows(p, seg)
        return p + mu_ref[i:i + 1, :] * (0.5 * (prev + nxt) - p)

    r = mixed(0)
    k = mixed(1)
    v = mixed(2)
    r_o[0] = r
    v_o[0] = v

    low = _dg(hb, wdn_ref[...])
    lwz = jnp.tanh(low[:, :2 * LORA]).astype(BF16)
    la = low[:, 2 * LORA:].astype(BF16)
    wpre = _dg(lwz, wdu_ref[...]) + w0_ref[...]
    apre = _dg(la, wiu_ref[...]) + ab_ref[...]
    lw = -math.exp(-0.5) * jax.nn.sigmoid(wpre)
    a = jax.nn.sigmoid(apre)
    lw0_o[0] = lw[:, :A_WIDTH]
    lw1_o[0] = lw[:, A_WIDTH:]

    kks = k * kks_ref[...]
    ss = _segsum(kks * kks, ones_ref)
    kk = kks * lax.rsqrt(jnp.maximum(ss, 1e-24))
    kk_o[0] = kk
    kas = kas_ref[...]
    a0 = a[:, :A_WIDTH]
    a1 = a[:, A_WIDTH:]
    kd0 = k * (1.0 + (a0 - 1.0) * kas)
    kd1 = k * (1.0 + (a1 - 1.0) * kas)
    kd0_o[0] = kd0
    kd1_o[0] = kd1
    b0_o[0] = kk * a0
    b1_o[0] = kk * a1
    bonus = _segsum(r * (kd0 + kd1) * brk_ref[...], ones_ref)
    bv_o[0] = bonus * v

    ga = proj(3)
    ga_o[0] = ga * jax.nn.sigmoid(ga)
    bgate = proj(4)
    cgate = proj(5)
    uconv = proj(6)
    gb = proj(7)
    cu_o[0] = cgate * uconv
    bg_o[0] = bgate * (gb * jax.nn.sigmoid(gb))


def _const_spec(shape):
    nd = len(shape)
    return pl.BlockSpec(shape, lambda *_: (0,) * nd, pipeline_mode=pl.Buffered(1))


def _pre(x, mod, weights, seg):
    B, T, _ = x.shape
    nt = T // TOKEN_TILE
    tok = lambda w: pl.BlockSpec((1, TOKEN_TILE, w), lambda b, t: (b, t, 0))
    mod_spec = pl.BlockSpec((1, 1, 3 * D_MODEL), (lambda b, t: (b, 0, 0)) if mod.shape[0] > 1 else (lambda b, t: (0, 0, 0)))
    n_out = 13
    return pl.pallas_call(
        functools.partial(_pre_kernel, seg=seg),
        grid=(B, nt),
        in_specs=[tok(D_MODEL), mod_spec] + [_const_spec(w.shape) for w in weights],
        out_specs=[tok(A_WIDTH)] * n_out,
        out_shape=[jax.ShapeDtypeStruct((B, T, A_WIDTH), F32)] * n_out,
        compiler_params=pltpu.CompilerParams(dimension_semantics=("parallel", "parallel"),
                                             vmem_limit_bytes=VMEM_LIMIT),
        name="pre",
    )(x, mod, *weights)


def _wkv_group(r, kk, v, lw, kd, b, h0, bwd):
    L = CHUNK
    t = lax.broadcasted_iota(jnp.int32, (L, GROUP), 0)
    lane = lax.broadcasted_iota(jnp.int32, (L, GROUP), 1)
    s = lane & (L - 1)
    if bwd:
        strict, incl = s > t, s >= t
    else:
        strict, incl = s < t, s <= t
    blk = (t >> 3) == (s >> 3)
    eye = (s == t).astype(F32)
    head_masks = [((lane >> 6) == hh).astype(F32).astype(BF16) for hh in range(HEADS_PER_GROUP)]

    def bd(q):
        qh, ql = _split2(q)
        return (jnp.concatenate([qh * m for m in head_masks], axis=0),
                jnp.concatenate([ql * m for m in head_masks], axis=0))

    def bdmm(p, w, nt=False):
        wh, wl = w
        ph, plo = _split2(p)
        m = p.shape[0]
        both = _dg(jnp.concatenate([ph, plo], axis=0), wh, nt)
        return both[:m] + both[m:] + _dg(ph, wl, nt)

    t2 = lax.broadcasted_iota(jnp.int32, (L, L), 0)
    s2 = lax.broadcasted_iota(jnp.int32, (L, L), 1)
    tri = ((s2 >= t2) if bwd else (s2 <= t2)).astype(F32).astype(BF16)
    l1 = lw.astype(BF16)
    rem = lw - l1.astype(F32)
    l2 = rem.astype(BF16)
    l3 = (rem - l2.astype(F32)).astype(BF16)
    c = _dg(tri, l1) + _dg(tri, l2) + _dg(tri, l3)
    c_last = c[0:1, :] if bwd else c[L - 1:L, :]
    e_neg = jnp.exp(-c)
    kt = kk * jnp.exp(c - lw)
    rt = r * jnp.exp(c)
    kdt = kd * e_neg
    bt = b * e_neg
    eye_g = eye * jnp.exp(c_last)

    lhs3 = jnp.concatenate([kt, rt, eye_g], axis=0)
    gk = bdmm(lhs3, bd(kdt), nt=True)
    gb = bdmm(lhs3, bd(bt), nt=True)
    a_k = jnp.where(strict, gk[:L], 0.0)
    a_rk = jnp.where(incl, gk[L:2 * L], 0.0)
    kd_t = gk[2 * L:]
    a_b = jnp.where(strict, gb[:L], 0.0)
    a_rb = jnp.where(incl, gb[L:2 * L], 0.0)
    bt_t = gb[2 * L:]

    d = jnp.where(blk, a_b, 0.0)
    e = a_b - d
    d2 = bdmm(d, bd(d))
    d4 = bdmm(d2, bd(d2))
    x1 = bdmm(eye - d, bd(eye + d2))
    t_d = bdmm(x1, bd(eye + d4))
    w_td = bd(t_d)
    n1 = bdmm(t_d, bd(e))
    n2 = bdmm(n1, bd(n1))
    n4 = bdmm(n2, bd(n2))
    x2 = bdmm(eye - n1, bd(eye + n2))
    f = bdmm(x2, bd(eye + n4))
    t_inv = bdmm(f, w_td)

    gh = bdmm(lhs3, bd(h0))
    gv = bdmm(jnp.concatenate([a_k, a_rk, kd_t], axis=0), bd(v))
    u = bdmm(t_inv, bd(gh[:L] + gv[:L]))
    gu = bdmm(jnp.concatenate([a_rb, bt_t], axis=0), bd(u))
    y = gh[L:2 * L] + gv[L:2 * L] - gu[:L]
    h_next = gh[2 * L:] + gv[2 * L:] - gu[L:]
    return y, h_next


def _wkv_kernel(rf, kkf, vf, lwf, kdf, bf, rb, kkb, vb, lwb, kdb, bb, s0_ref, yf_ref, yb_ref, st_ref, h_ref):
    c = pl.program_id(1)

    @pl.when(c == 0)
    def _():
        h_ref[...] = s0_ref[0]

    for d, (r_, kk_, v_, lw_, kd_, b_, y_) in enumerate(((rf, kkf, vf, lwf, kdf, bf, yf_ref),
                                                         (rb, kkb, vb, lwb, kdb, bb, yb_ref))):
        for g in range(N_GROUPS):
            sl = slice(g * GROUP, (g + 1) * GROUP)
            y, hn = _wkv_group(r_[0, :, sl], kk_[0, :, sl], v_[0, :, sl], lw_[0, :, sl], kd_[0, :, sl],
                               b_[0, :, sl], h_ref[d, :, sl], bwd=(d == 1))
            y_[0, :, sl] = y
            h_ref[d, :, sl] = hn

    @pl.when(c == pl.num_programs(1) - 1)
    def _():
        st_ref[0] = h_ref[...]


def _wkv(r, kk, v, lw0, lw1, kd0, kd1, b0, b1, s0):
    B, T, _ = r.shape
    nc = T // CHUNK
    fwd = pl.BlockSpec((1, CHUNK, A_WIDTH), lambda b, c: (b, c, 0))
    bwd = pl.BlockSpec((1, CHUNK, A_WIDTH), lambda b, c: (b, nc - 1 - c, 0))
    st = pl.BlockSpec((1, 2, HEAD_DIM, A_WIDTH), lambda b, c: (b, 0, 0, 0))
    return pl.pallas_call(
        _wkv_kernel,
        grid=(B, nc),
        in_specs=[fwd] * 6 + [bwd] * 6 + [st],
        out_specs=[fwd, bwd, st],
        out_shape=[jax.ShapeDtypeStruct((B, T, A_WIDTH), F32), jax.ShapeDtypeStruct((B, T, A_WIDTH), F32),
                   jax.ShapeDtypeStruct((B, 2, HEAD_DIM, A_WIDTH), F32)],
        scratch_shapes=[pltpu.VMEM((2, HEAD_DIM, A_WIDTH), F32)],
        compiler_params=pltpu.CompilerParams(dimension_semantics=("parallel", "arbitrary"),
                                             vmem_limit_bytes=VMEM_LIMIT),
        name="wkv",
    )(r, kk, v, lw0, kd0, b0, r, kk, v, lw1, kd1, b1, s0)


def _post_kernel(*refs, latent):
    if latent:
        (x_ref, mod_ref, yf_ref, yb_ref, bv_ref, ga_ref, cu_ref, cup_ref, cun_ref, bg_ref,
         gnw_ref, gnb_ref, cw_ref, wout_ref, fg_ref, ones_ref, o_ref) = refs
    else:
        (x_ref, mod_ref, yf_ref, yb_ref, bv_ref, ga_ref, cu_ref, bg_ref,
         gnw_ref, gnb_ref, cw_ref, wout_ref, fg_ref, ones_ref, o_ref) = refs
    x = x_ref[0]
    gate = mod_ref[0][:, 2 * D_MODEL:]
    y = yf_ref[0] + yb_ref[0]
    inv_n = 1.0 / HEAD_DIM
    mean = _segsum(y, ones_ref) * inv_n
    dlt = y - mean
    var = _segsum(dlt * dlt, ones_ref) * inv_n
    yn = dlt * lax.rsqrt(var + GN_EPS) * gnw_ref[...] + gnb_ref[...] + bv_ref[0]
    ya = yn * ga_ref[0]

    cu = cu_ref[0]
    w_prev, w_mid, w_next = cw_ref[0:1, :], cw_ref[1:2, :], cw_ref[2:3, :]
    if latent:
        half = B_WIDTH // 2
        t = pl.program_id(1)
        ph, nh = _shift_rows(cu[:, :half], GRID_W)
        top = jnp.where(t == 0, 0.0, cup_ref[0][:, half:])
        bot = jnp.where(t == pl.num_programs(1) - 1, 0.0, cun_ref[0][:, half:])
        pv = jnp.concatenate([top, cu[:TOKEN_TILE - HALO, half:]], axis=0)
        nv = jnp.concatenate([cu[HALO:, half:], bot], axis=0)
        prev = jnp.concatenate([ph, pv], axis=1)
        nxt = jnp.concatenate([nh, nv], axis=1)
    else:
        prev, nxt = _shift_rows(cu, TOKEN_TILE)
    yb = bg_ref[0] * (w_prev * prev + w_mid * cu + w_next * nxt)

    u = _dg(ya.astype(BF16), wout_ref[:A_WIDTH, :]) + _dg(yb.astype(BF16), wout_ref[A_WIDTH:, :])
    o = x + gate * u
    o_ref[0] = o * lax.rsqrt(jnp.mean(o * o, axis=-1, keepdims=True) + NORM_EPS) * fg_ref[...]


def _post(x, mod, yf, yb, bv, ga, cu, bg, weights, latent):
    B, T, _ = x.shape
    nt = T // TOKEN_TILE
    tok = lambda w: pl.BlockSpec((1, TOKEN_TILE, w), lambda b, t: (b, t, 0))
    mod_spec = pl.BlockSpec((1, 1, 3 * D_MODEL), (lambda b, t: (b, 0, 0)) if mod.shape[0] > 1 else (lambda b, t: (0, 0, 0)))
    per_tile = TOKEN_TILE // HALO
    n_halo = T // HALO
    halo_prev = pl.BlockSpec((1, HALO, B_WIDTH), lambda b, t: (b, jnp.maximum(t * per_tile - 1, 0), 0))
    halo_next = pl.BlockSpec((1, HALO, B_WIDTH), lambda b, t: (b, jnp.minimum((t + 1) * per_tile, n_halo - 1), 0))
    acts = [x, mod, yf, yb, bv, ga, cu] + ([cu, cu] if latent else []) + [bg]
    specs = [tok(D_MODEL), mod_spec] + [tok(A_WIDTH)] * 5 + ([halo_prev, halo_next] if latent else []) + [tok(B_WIDTH)]
    return pl.pallas_call(
        functools.partial(_post_kernel, latent=latent),
        grid=(B, nt),
        in_specs=specs + [_const_spec(w.shape) for w in weights],
        out_specs=tok(D_MODEL),
        out_shape=jax.ShapeDtypeStruct((B, T, D_MODEL), F32),
        compiler_params=pltpu.CompilerParams(dimension_semantics=("parallel", "parallel"),
                                             vmem_limit_bytes=VMEM_LIMIT),
        name="post",
    )(*acts, *weights)


def _block_diag2(w):
    z = jnp.zeros_like(w[0])
    return jnp.concatenate([jnp.concatenate([w[0], z], axis=1), jnp.concatenate([z, w[1]], axis=1)], axis=0)


def kernel(x_prompt, x_sample, c, state_wkv, c_ctx, w_ada, b_ada, norm_g, w_in, shift_mu, decay_w0, decay_down,
           decay_up, iclr_bias, iclr_down, iclr_up, kk_scale, ka_scale, bonus_rk, gn_w, gn_b, conv_w, w_out, final_g):
    assert w_in.shape[0] == 1, "single layer"
    n_lat = c.shape[0]
    row = lambda z: z.reshape(1, -1)

    pad = (-(n_lat + 1)) % 8
    cvec = jnp.concatenate([c, c_ctx[None, :], jnp.zeros((pad, D_MODEL), F32)], axis=0)
    mod = _adaln(cvec, w_ada[0].astype(BF16), row(b_ada[0]))
    mod_lat = mod[:n_lat, None, :]
    mod_ctx = mod[n_lat:n_lat + 1, None, :]

    seg_id = np.arange(GROUP) // HEAD_DIM
    ones_bd = jnp.asarray((seg_id[:, None] == seg_id[None, :]).astype(np.float32), dtype=BF16)
    w_down = jnp.concatenate([decay_down[0, 0], decay_down[0, 1], iclr_down[0, 0], iclr_down[0, 1]], axis=1)
    pre_w = [row(norm_g[0]), w_in[0].astype(BF16), w_down.astype(BF16), _block_diag2(decay_up[0]).astype(BF16),
             _block_diag2(iclr_up[0]).astype(BF16), row(decay_w0[0]), row(iclr_bias[0]), shift_mu[0],
             row(kk_scale[0]), row(ka_scale[0]), row(bonus_rk[0]), ones_bd]
    post_w = [row(gn_w[0]), row(gn_b[0]), conv_w[0], w_out[0].astype(BF16), row(final_g), ones_bd]

    def to_state(s):
        return jnp.transpose(s, (0, 1, 4, 2, 3)).reshape(s.shape[0], 2, HEAD_DIM, A_WIDTH)

    def from_state(s):
        return jnp.transpose(s.reshape(s.shape[0], 2, HEAD_DIM, A_HEADS, HEAD_DIM), (0, 1, 3, 4, 2))

    def layer(x, mod_x, s0, latent):
        seg = GRID_W if latent else x.shape[1]
        r, v, kk, lw0, lw1, kd0, kd1, b0, b1, bv, ga, cu, bg = _pre(x, mod_x, pre_w, seg)
        yf, yb, st = _wkv(r, kk, v, lw0, lw1, kd0, kd1, b0, b1, s0)
        return _post(x, mod_x, yf, yb, bv, ga, cu, bg, post_w, latent), st

    assert x_prompt.shape[1] == TOKEN_TILE, "context sequences are one token tile"
    s_zero = jnp.zeros((x_prompt.shape[0], 2, HEAD_DIM, A_WIDTH), F32)
    y_prompt, s_ctx = layer(x_prompt, mod_ctx, s_zero, False)
    y_sample, _ = layer(x_sample, mod_lat, to_state(state_wkv[:, 0]), True)
    new_state = from_state(s_ctx)[:, None]
    return (y_prompt, y_sample, new_state)
```

```python
import functools
import math

import numpy as np
import jax
import jax.numpy as jnp
from jax import lax
from jax.experimental import pallas as pl
from jax.experimental.pallas import tpu as pltpu

D_MODEL = 1024
A_WIDTH = 512
B_WIDTH = 512
HEAD_DIM = 64
A_HEADS = 8
GRID_W = 64
LORA = 64
IN_WIDTH = 4 * A_WIDTH + 4 * B_WIDTH
NORM_EPS = 1e-6
GN_EPS = 64e-5

CHUNK = 64
GROUP = 256
HEADS_PER_GROUP = GROUP // HEAD_DIM
N_GROUPS = A_WIDTH // GROUP
TOKEN_TILE = 256
HALO = GRID_W
VMEM_LIMIT = 56 * 1024 * 1024

F32 = jnp.float32
BF16 = jnp.bfloat16


def _split2(x):
    hi = x.astype(BF16)
    lo = (x - hi.astype(F32)).astype(BF16)
    return hi, lo


def _dg(a, b, nt=False):
    dims = (((1,), (1 if nt else 0,)), ((), ()))
    return lax.dot_general(a, b, dims, preferred_element_type=F32)


def _segsum(z, ones_ref):
    outs = []
    for g in range(N_GROUPS):
        zh, zl = _split2(z[:, g * GROUP:(g + 1) * GROUP])
        outs.append(_dg(zh, ones_ref[...]) + _dg(zl, ones_ref[...]))
    return jnp.concatenate(outs, axis=1)


def _adaln_kernel(c_ref, w_ref, b_ref, o_ref):
    c = c_ref[...]
    s = c * jax.nn.sigmoid(c)
    o_ref[...] = _dg(s.astype(BF16), w_ref[...]) + b_ref[...]


def _adaln(cvec, w_ada_bf, b_ada):
    rows = cvec.shape[0]
    return pl.pallas_call(
        _adaln_kernel,
        out_shape=jax.ShapeDtypeStruct((rows, 3 * D_MODEL), F32),
        compiler_params=pltpu.CompilerParams(vmem_limit_bytes=VMEM_LIMIT),
        name="adaln",
    )(cvec, w_ada_bf, b_ada)


def _shift_rows(p, seg):
    rows = p.shape[0]
    assert seg & (seg - 1) == 0
    pos = lax.broadcasted_iota(jnp.int32, p.shape, 0) & (seg - 1)
    prev = jnp.where(pos == 0, 0.0, pltpu.roll(p, 1, 0))
    nxt = jnp.where(pos == seg - 1, 0.0, pltpu.roll(p, rows - 1, 0))
    return prev, nxt


def _pre_kernel(x_ref, mod_ref, g_ref, win_ref, wdn_ref, wdu_ref, wiu_ref, w0_ref, ab_ref, mu_ref,
                kks_ref, kas_ref, brk_ref, ones_ref,
                r_o, v_o, kk_o, lw0_o, lw1_o, kd0_o, kd1_o, b0_o, b1_o, bv_o, ga_o, cu_o, bg_o, *, seg):
    x = x_ref[0]
    mod = mod_ref[0]
    shift = mod[:, :D_MODEL]
    scale = mod[:, D_MODEL:2 * D_MODEL]
    xn = x * lax.rsqrt(jnp.mean(x * x, axis=-1, keepdims=True) + NORM_EPS) * g_ref[...]
    h = xn * (1.0 + scale) + shift
    hb = h.astype(BF16)

    def proj(i):
        return _dg(hb, win_ref[:, i * A_WIDTH:(i + 1) * A_WIDTH])

    def mixed(i):
        p = proj(i)
        prev, nxt = _shift_rows(p, seg)
        return p + mu_ref[i:i + 1, :] * (0.5 * (prev + nxt) - p)

    r = mixed(0)
    k = mixed(1)
    v = mixed(2)
    r_o[0] = r
    v_o[0] = v

    low = _dg(hb, wdn_ref[...])
    lwz = jnp.tanh(low[:, :2 * LORA]).astype(BF16)
    la = low[:, 2 * LORA:].astype(BF16)
    wpre = _dg(lwz, wdu_ref[...]) + w0_ref[...]
    apre = _dg(la, wiu_ref[...]) + ab_ref[...]
    lw = -math.exp(-0.5) * jax.nn.sigmoid(wpre)
    a = jax.nn.sigmoid(apre)
    lw0_o[0] = lw[:, :A_WIDTH]
    lw1_o[0] = lw[:, A_WIDTH:]

    kks = k * kks_ref[...]
    ss = _segsum(kks * kks, ones_ref)
    kk = kks * lax.rsqrt(jnp.maximum(ss, 1e-24))
    kk_o[0] = kk
    kas = kas_ref[...]
    a0 = a[:, :A_WIDTH]
    a1 = a[:, A_WIDTH:]
    kd0 = k * (1.0 + (a0 - 1.0) * kas)
    kd1 = k * (1.0 + (a1 - 1.0) * kas)
    kd0_o[0] = kd0
    kd1_o[0] = kd1
    b0_o[0] = kk * a0
    b1_o[0] = kk * a1
    bonus = _segsum(r * (kd0 + kd1) * brk_ref[...], ones_ref)
    bv_o[0] = bonus * v

    ga = proj(3)
    ga_o[0] = ga * jax.nn.sigmoid(ga)
    bgate = proj(4)
    cgate = proj(5)
    uconv = proj(6)
    gb = proj(7)
    cu_o[0] = cgate * uconv
    bg_o[0] = bgate * (gb * jax.nn.sigmoid(gb))


def _const_spec(shape):
    nd = len(shape)
    return pl.BlockSpec(shape, lambda *_: (0,) * nd, pipeline_mode=pl.Buffered(1))


def _pre(x, mod, weights, seg):
    B, T, _ = x.shape
    nt = T // TOKEN_TILE
    tok = lambda w: pl.BlockSpec((1, TOKEN_TILE, w), lambda b, t: (b, t, 0))
    mod_spec = pl.BlockSpec((1, 1, 3 * D_MODEL), (lambda b, t: (b, 0, 0)) if mod.shape[0] > 1 else (lambda b, t: (0, 0, 0)))
    n_out = 13
    return pl.pallas_call(
        functools.partial(_pre_kernel, seg=seg),
        grid=(B, nt),
        in_specs=[tok(D_MODEL), mod_spec] + [_const_spec(w.shape) for w in weights],
        out_specs=[tok(A_WIDTH)] * n_out,
        out_shape=[jax.ShapeDtypeStruct((B, T, A_WIDTH), F32)] * n_out,
        compiler_params=pltpu.CompilerParams(dimension_semantics=("parallel", "parallel"),
                                             vmem_limit_bytes=VMEM_LIMIT),
        name="pre",
    )(x, mod, *weights)


def _wkv_streams(streams):
    L = CHUNK
    t = lax.broadcasted_iota(jnp.int32, (L, GROUP), 0)
    lane = lax.broadcasted_iota(jnp.int32, (L, GROUP), 1)
    s = lane & (L - 1)
    blk = (t >> 3) == (s >> 3)
    eye = (s == t).astype(F32)
    head_masks = [((lane >> 6) == hh).astype(F32).astype(BF16) for hh in range(HEADS_PER_GROUP)]
    t2 = lax.broadcasted_iota(jnp.int32, (L, L), 0)
    s2 = lax.broadcasted_iota(jnp.int32, (L, L), 1)
    tri_f = (s2 <= t2).astype(F32).astype(BF16)
    tri_b = (s2 >= t2).astype(F32).astype(BF16)

    def bd(q):
        qh, ql = _split2(q)
        return (jnp.concatenate([qh * m for m in head_masks], axis=0),
                jnp.concatenate([ql * m for m in head_masks], axis=0))

    def bdmm(p, w, nt=False):
        wh, wl = w
        ph, plo = _split2(p)
        m = p.shape[0]
        both = _dg(jnp.concatenate([ph, plo], axis=0), wh, nt)
        return both[:m] + both[m:] + _dg(ph, wl, nt)

    def each(fn, *cols):
        return [fn(*args) for args in zip(*cols)]

    r, kk, v, lw, kd, b, h0, bwd = (list(col) for col in zip(*streams))
    strict = [(s > t) if bw else (s < t) for bw in bwd]
    incl = [(s >= t) if bw else (s <= t) for bw in bwd]

    def cumsum(lw_, bw):
        tri = tri_b if bw else tri_f
        l1 = lw_.astype(BF16)
        rem = lw_ - l1.astype(F32)
        l2 = rem.astype(BF16)
        l3 = (rem - l2.astype(F32)).astype(BF16)
        return _dg(tri, l1) + _dg(tri, l2) + _dg(tri, l3)

    c = each(cumsum, lw, bwd)
    c_last = each(lambda c_, bw: c_[0:1, :] if bw else c_[L - 1:L, :], c, bwd)
    e_neg = each(lambda c_: jnp.exp(-c_), c)
    kt = each(lambda kk_, c_, lw_: kk_ * jnp.exp(c_ - lw_), kk, c, lw)
    rt = each(lambda r_, c_: r_ * jnp.exp(c_), r, c)
    kdt = each(lambda kd_, e_: kd_ * e_, kd, e_neg)
    bt = each(lambda b_, e_: b_ * e_, b, e_neg)
    eye_g = each(lambda cl: eye * jnp.exp(cl), c_last)
    lhs3 = each(lambda a1, a2, a3: jnp.concatenate([a1, a2, a3], axis=0), kt, rt, eye_g)

    gk = each(lambda p, q: bdmm(p, bd(q), nt=True), lhs3, kdt)
    gb = each(lambda p, q: bdmm(p, bd(q), nt=True), lhs3, bt)
    a_k = each(lambda m, g_: jnp.where(m, g_[:L], 0.0), strict, gk)
    a_rk = each(lambda m, g_: jnp.where(m, g_[L:2 * L], 0.0), incl, gk)
    a_b = each(lambda m, g_: jnp.where(m, g_[:L], 0.0), strict, gb)
    a_rb = each(lambda m, g_: jnp.where(m, g_[L:2 * L], 0.0), incl, gb)

    mm = lambda p, q: bdmm(p, bd(q))
    d = each(lambda a: jnp.where(blk, a, 0.0), a_b)
    e = each(lambda a, d_: a - d_, a_b, d)
    d2 = each(mm, d, d)
    d4 = each(mm, d2, d2)
    x1 = each(lambda d_, d2_: mm(eye - d_, eye + d2_), d, d2)
    t_d = each(lambda x_, d4_: mm(x_, eye + d4_), x1, d4)
    w_td = each(bd, t_d)
    n1 = each(mm, t_d, e)
    n2 = each(mm, n1, n1)
    n4 = each(mm, n2, n2)
    x2 = each(lambda n1_, n2_: mm(eye - n1_, eye + n2_), n1, n2)
    f = each(lambda x_, n4_: mm(x_, eye + n4_), x2, n4)
    t_inv = each(bdmm, f, w_td)

    gh = each(mm, lhs3, h0)
    gv = each(lambda ak, ark, g_, v_: mm(jnp.concatenate([ak, ark, g_[2 * L:]], axis=0), v_), a_k, a_rk, gk, v)
    u = each(lambda ti, gh_, gv_: mm(ti, gh_[:L] + gv_[:L]), t_inv, gh, gv)
    gu = each(lambda arb, g_, u_: mm(jnp.concatenate([arb, g_[2 * L:]], axis=0), u_), a_rb, gb, u)
    y = each(lambda gh_, gv_, gu_: gh_[L:2 * L] + gv_[L:2 * L] - gu_[:L], gh, gv, gu)
    h_next = each(lambda gh_, gv_, gu_: gh_[2 * L:] + gv_[2 * L:] - gu_[L:], gh, gv, gu)
    return list(zip(y, h_next))


def _wkv_kernel(rf, kkf, vf, lwf, kdf, bf, rb, kkb, vb, lwb, kdb, bb, s0_ref, yf_ref, yb_ref, st_ref, h_ref):
    c = pl.program_id(1)

    @pl.when(c == 0)
    def _():
        h_ref[...] = s0_ref[0]

    dirs = ((rf, kkf, vf, lwf, kdf, bf, yf_ref), (rb, kkb, vb, lwb, kdb, bb, yb_ref))
    groups = [slice(g * GROUP, (g + 1) * GROUP) for g in range(N_GROUPS)]
    keys = [(d, sl) for d in range(2) for sl in groups]
    streams = [tuple(ref[0, :, sl] for ref in dirs[d][:6]) + (h_ref[d, :, sl], d == 1) for d, sl in keys]
    for (d, sl), (y, hn) in zip(keys, _wkv_streams(streams)):
        dirs[d][6][0, :, sl] = y
        h_ref[d, :, sl] = hn

    @pl.when(c == pl.num_programs(1) - 1)
    def _():
        st_ref[0] = h_ref[...]


def _wkv(r, kk, v, lw0, lw1, kd0, kd1, b0, b1, s0):
    B, T, _ = r.shape
    nc = T // CHUNK
    fwd = pl.BlockSpec((1, CHUNK, A_WIDTH), lambda b, c: (b, c, 0))
    bwd = pl.BlockSpec((1, CHUNK, A_WIDTH), lambda b, c: (b, nc - 1 - c, 0))
    st = pl.BlockSpec((1, 2, HEAD_DIM, A_WIDTH), lambda b, c: (b, 0, 0, 0))
    return pl.pallas_call(
        _wkv_kernel,
        grid=(B, nc),
        in_specs=[fwd] * 6 + [bwd] * 6 + [st],
        out_specs=[fwd, bwd, st],
        out_shape=[jax.ShapeDtypeStruct((B, T, A_WIDTH), F32), jax.ShapeDtypeStruct((B, T, A_WIDTH), F32),
                   jax.ShapeDtypeStruct((B, 2, HEAD_DIM, A_WIDTH), F32)],
        scratch_shapes=[pltpu.VMEM((2, HEAD_DIM, A_WIDTH), F32)],
        compiler_params=pltpu.CompilerParams(dimension_semantics=("parallel", "arbitrary"),
                                             vmem_limit_bytes=VMEM_LIMIT),
        name="wkv",
    )(r, kk, v, lw0, kd0, b0, r, kk, v, lw1, kd1, b1, s0)


def _post_kernel(*refs, latent):
    if latent:
        (x_ref, mod_ref, yf_ref, yb_ref, bv_ref, ga_ref, cu_ref, cup_ref, cun_ref, bg_ref,
         gnw_ref, gnb_ref, cw_ref, wout_ref, fg_ref, ones_ref, o_ref) = refs
    else:
        (x_ref, mod_ref, yf_ref, yb_ref, bv_ref, ga_ref, cu_ref, bg_ref,
         gnw_ref, gnb_ref, cw_ref, wout_ref, fg_ref, ones_ref, o_ref) = refs
    x = x_ref[0]
    gate = mod_ref[0][:, 2 * D_MODEL:]
    y = yf_ref[0] + yb_ref[0]
    inv_n = 1.0 / HEAD_DIM
    mean = _segsum(y, ones_ref) * inv_n
    dlt = y - mean
    var = _segsum(dlt * dlt, ones_ref) * inv_n
    yn = dlt * lax.rsqrt(var + GN_EPS) * gnw_ref[...] + gnb_ref[...] + bv_ref[0]
    ya = yn * ga_ref[0]

    cu = cu_ref[0]
    w_prev, w_mid, w_next = cw_ref[0:1, :], cw_ref[1:2, :], cw_ref[2:3, :]
    if latent:
        half = B_WIDTH // 2
        t = pl.program_id(1)
        ph, nh = _shift_rows(cu[:, :half], GRID_W)
        top = jnp.where(t == 0, 0.0, cup_ref[0][:, half:])
        bot = jnp.where(t == pl.num_programs(1) - 1, 0.0, cun_ref[0][:, half:])
        pv = jnp.concatenate([top, cu[:TOKEN_TILE - HALO, half:]], axis=0)
        nv = jnp.concatenate([cu[HALO:, half:], bot], axis=0)
        prev = jnp.concatenate([ph, pv], axis=1)
        nxt = jnp.concatenate([nh, nv], axis=1)
    else:
        prev, nxt = _shift_rows(cu, TOKEN_TILE)
    yb = bg_ref[0] * (w_prev * prev + w_mid * cu + w_next * nxt)

    u = _dg(ya.astype(BF16), wout_ref[:A_WIDTH, :]) + _dg(yb.astype(BF16), wout_ref[A_WIDTH:, :])
    o = x + gate * u
    o_ref[0] = o * lax.rsqrt(jnp.mean(o * o, axis=-1, keepdims=True) + NORM_EPS) * fg_ref[...]


def _post(x, mod, yf, yb, bv, ga, cu, bg, weights, latent):
    B, T, _ = x.shape
    nt = T // TOKEN_TILE
    tok = lambda w: pl.BlockSpec((1, TOKEN_TILE, w), lambda b, t: (b, t, 0))
    mod_spec = pl.BlockSpec((1, 1, 3 * D_MODEL), (lambda b, t: (b, 0, 0)) if mod.shape[0] > 1 else (lambda b, t: (0, 0, 0)))
    per_tile = TOKEN_TILE // HALO
    n_halo = T // HALO
    halo_prev = pl.BlockSpec((1, HALO, B_WIDTH), lambda b, t: (b, jnp.maximum(t * per_tile - 1, 0), 0))
    halo_next = pl.BlockSpec((1, HALO, B_WIDTH), lambda b, t: (b, jnp.minimum((t + 1) * per_tile, n_halo - 1), 0))
    acts = [x, mod, yf, yb, bv, ga, cu] + ([cu, cu] if latent else []) + [bg]
    specs = [tok(D_MODEL), mod_spec] + [tok(A_WIDTH)] * 5 + ([halo_prev, halo_next] if latent else []) + [tok(B_WIDTH)]
    return pl.pallas_call(
        functools.partial(_post_kernel, latent=latent),
        grid=(B, nt),
        in_specs=specs + [_const_spec(w.shape) for w in weights],
        out_specs=tok(D_MODEL),
        out_shape=jax.ShapeDtypeStruct((B, T, D_MODEL), F32),
        compiler_params=pltpu.CompilerParams(dimension_semantics=("parallel", "parallel"),
                                             vmem_limit_bytes=VMEM_LIMIT),
        name="post",
    )(*acts, *weights)


def _block_diag2(w):
    z = jnp.zeros_like(w[0])
    return jnp.concatenate([jnp.concatenate([w[0], z], axis=1), jnp.concatenate([z, w[1]], axis=1)], axis=0)


def kernel(x_prompt, x_sample, c, state_wkv, c_ctx, w_ada, b_ada, norm_g, w_in, shift_mu, decay_w0, decay_down,
           decay_up, iclr_bias, iclr_down, iclr_up, kk_scale, ka_scale, bonus_rk, gn_w, gn_b, conv_w, w_out, final_g):
    assert w_in.shape[0] == 1, "single layer"
    n_lat = c.shape[0]
    row = lambda z: z.reshape(1, -1)

    pad = (-(n_lat + 1)) % 8
    cvec = jnp.concatenate([c, c_ctx[None, :], jnp.zeros((pad, D_MODEL), F32)], axis=0)
    mod = _adaln(cvec, w_ada[0].astype(BF16), row(b_ada[0]))
    mod_lat = mod[:n_lat, None, :]
    mod_ctx = mod[n_lat:n_lat + 1, None, :]

    seg_id = np.arange(GROUP) // HEAD_DIM
    ones_bd = jnp.asarray((seg_id[:, None] == seg_id[None, :]).astype(np.float32), dtype=BF16)
    w_down = jnp.concatenate([decay_down[0, 0], decay_down[0, 1], iclr_down[0, 0], iclr_down[0, 1]], axis=1)
    pre_w = [row(norm_g[0]), w_in[0].astype(BF16), w_down.astype(BF16), _block_diag2(decay_up[0]).astype(BF16),
             _block_diag2(iclr_up[0]).astype(BF16), row(decay_w0[0]), row(iclr_bias[0]), shift_mu[0],
             row(kk_scale[0]), row(ka_scale[0]), row(bonus_rk[0]), ones_bd]
    post_w = [row(gn_w[0]), row(gn_b[0]), conv_w[0], w_out[0].astype(BF16), row(final_g), ones_bd]

    def to_state(s):
        return jnp.transpose(s, (0, 1, 4, 2, 3)).reshape(s.shape[0], 2, HEAD_DIM, A_WIDTH)

    def from_state(s):
        return jnp.transpose(s.reshape(s.shape[0], 2, HEAD_DIM, A_HEADS, HEAD_DIM), (0, 1, 3, 4, 2))

    def layer(x, mod_x, s0, latent):
        seg = GRID_W if latent else x.shape[1]
        r, v, kk, lw0, lw1, kd0, kd1, b0, b1, bv, ga, cu, bg = _pre(x, mod_x, pre_w, seg)
        yf, yb, st = _wkv(r, kk, v, lw0, lw1, kd0, kd1, b0, b1, s0)
        return _post(x, mod_x, yf, yb, bv, ga, cu, bg, post_w, latent), st

    assert x_prompt.shape[1] == TOKEN_TILE, "context sequences are one token tile"
    s_zero = jnp.zeros((x_prompt.shape[0], 2, HEAD_DIM, A_WIDTH), F32)
    y_prompt, s_ctx = layer(x_prompt, mod_ctx, s_zero, False)
    y_sample, _ = layer(x_sample, mod_lat, to_state(state_wkv[:, 0]), True)
    new_state = from_state(s_ctx)[:, None]
    return (y_prompt, y_sample, new_state)
```

```python
import functools
import math

import numpy as np
import jax
import jax.numpy as jnp
from jax import lax
from jax.experimental import pallas as pl
from jax.experimental.pallas import tpu as pltpu

D_MODEL = 1024
A_WIDTH = 512
B_WIDTH = 512
HEAD_DIM = 64
A_HEADS = 8
GRID_W = 64
LORA = 64
IN_WIDTH = 4 * A_WIDTH + 4 * B_WIDTH
NORM_EPS = 1e-6
GN_EPS = 64e-5

CHUNK = 64
GROUP = 256
HEADS_PER_GROUP = GROUP // HEAD_DIM
N_GROUPS = A_WIDTH // GROUP
WKV_BATCH = 4
TOKEN_TILE = 256
HALO = GRID_W
VMEM_LIMIT = 56 * 1024 * 1024

F32 = jnp.float32
BF16 = jnp.bfloat16


def _split2(x):
    hi = x.astype(BF16)
    lo = (x - hi.astype(F32)).astype(BF16)
    return hi, lo


def _dg(a, b, nt=False):
    dims = (((1,), (1 if nt else 0,)), ((), ()))
    return lax.dot_general(a, b, dims, preferred_element_type=F32)


def _segsum(z, ones_ref):
    outs = []
    for g in range(N_GROUPS):
        zh, zl = _split2(z[:, g * GROUP:(g + 1) * GROUP])
        outs.append(_dg(zh, ones_ref[...]) + _dg(zl, ones_ref[...]))
    return jnp.concatenate(outs, axis=1)


def _adaln_kernel(c_ref, w_ref, b_ref, o_ref):
    c = c_ref[...]
    s = c * jax.nn.sigmoid(c)
    o_ref[...] = _dg(s.astype(BF16), w_ref[...]) + b_ref[...]


def _adaln(cvec, w_ada_bf, b_ada):
    rows = cvec.shape[0]
    return pl.pallas_call(
        _adaln_kernel,
        out_shape=jax.ShapeDtypeStruct((rows, 3 * D_MODEL), F32),
        compiler_params=pltpu.CompilerParams(vmem_limit_bytes=VMEM_LIMIT),
        name="adaln",
    )(cvec, w_ada_bf, b_ada)


def _shift_rows(p, seg):
    rows = p.shape[0]
    assert seg & (seg - 1) == 0
    pos = lax.broadcasted_iota(jnp.int32, p.shape, 0) & (seg - 1)
    prev = jnp.where(pos == 0, 0.0, pltpu.roll(p, 1, 0))
    nxt = jnp.where(pos == seg - 1, 0.0, pltpu.roll(p, rows - 1, 0))
    return prev, nxt


def _pre_kernel(x_ref, mod_ref, g_ref, win_ref, wdn_ref, wdu_ref, wiu_ref, w0_ref, ab_ref, mu_ref,
                kks_ref, kas_ref, brk_ref, ones_ref,
                r_o, v_o, kk_o, lw0_o, lw1_o, kd0_o, kd1_o, b0_o, b1_o, bv_o, ga_o, cu_o, bg_o, *, seg):
    x = x_ref[0]
    mod = mod_ref[0]
    shift = mod[:, :D_MODEL]
    scale = mod[:, D_MODEL:2 * D_MODEL]
    xn = x * lax.rsqrt(jnp.mean(x * x, axis=-1, keepdims=True) + NORM_EPS) * g_ref[...]
    h = xn * (1.0 + scale) + shift
    hb = h.astype(BF16)

    def proj(i):
        return _dg(hb, win_ref[:, i * A_WIDTH:(i + 1) * A_WIDTH])

    def mixed(i):
        p = proj(i)
        prev, nxt = _shift_rows(p, seg)
        return p + mu_ref[i:i + 1, :] * (0.5 * (prev + nxt) - p)

    r = mixed(0)
    k = mixed(1)
    v = mixed(2)
    r_o[0] = r
    v_o[0] = v

    low = _dg(hb, wdn_ref[...])
    lwz = jnp.tanh(low[:, :2 * LORA]).astype(BF16)
    la = low[:, 2 * LORA:].astype(BF16)
    wpre = _dg(lwz, wdu_ref[...]) + w0_ref[...]
    apre = _dg(la, wiu_ref[...]) + ab_ref[...]
    lw = -math.exp(-0.5) * jax.nn.sigmoid(wpre)
    a = jax.nn.sigmoid(apre)
    lw0_o[0] = lw[:, :A_WIDTH]
    lw1_o[0] = lw[:, A_WIDTH:]

    kks = k * kks_ref[...]
    ss = _segsum(kks * kks, ones_ref)
    kk = kks * lax.rsqrt(jnp.maximum(ss, 1e-24))
    kk_o[0] = kk
    kas = kas_ref[...]
    a0 = a[:, :A_WIDTH]
    a1 = a[:, A_WIDTH:]
    kd0 = k * (1.0 + (a0 - 1.0) * kas)
    kd1 = k * (1.0 + (a1 - 1.0) * kas)
    kd0_o[0] = kd0
    kd1_o[0] = kd1
    b0_o[0] = kk * a0
    b1_o[0] = kk * a1
    bonus = _segsum(r * (kd0 + kd1) * brk_ref[...], ones_ref)
    bv_o[0] = bonus * v

    ga = proj(3)
    ga_o[0] = ga * jax.nn.sigmoid(ga)
    bgate = proj(4)
    cgate = proj(5)
    uconv = proj(6)
    gb = proj(7)
    cu_o[0] = cgate * uconv
    bg_o[0] = bgate * (gb * jax.nn.sigmoid(gb))


def _const_spec(shape):
    nd = len(shape)
    return pl.BlockSpec(shape, lambda *_: (0,) * nd, pipeline_mode=pl.Buffered(1))


def _pre(x, mod, weights, seg):
    B, T, _ = x.shape
    nt = T // TOKEN_TILE
    tok = lambda w: pl.BlockSpec((1, TOKEN_TILE, w), lambda b, t: (b, t, 0))
    mod_spec = pl.BlockSpec((1, 1, 3 * D_MODEL), (lambda b, t: (b, 0, 0)) if mod.shape[0] > 1 else (lambda b, t: (0, 0, 0)))
    n_out = 13
    return pl.pallas_call(
        functools.partial(_pre_kernel, seg=seg),
        grid=(B, nt),
        in_specs=[tok(D_MODEL), mod_spec] + [_const_spec(w.shape) for w in weights],
        out_specs=[tok(A_WIDTH)] * n_out,
        out_shape=[jax.ShapeDtypeStruct((B, T, A_WIDTH), F32)] * n_out,
        compiler_params=pltpu.CompilerParams(dimension_semantics=("parallel", "parallel"),
                                             vmem_limit_bytes=VMEM_LIMIT),
        name="pre",
    )(x, mod, *weights)


def _wkv_streams(streams):
    L = CHUNK
    t = lax.broadcasted_iota(jnp.int32, (L, GROUP), 0)
    lane = lax.broadcasted_iota(jnp.int32, (L, GROUP), 1)
    s = lane & (L - 1)
    eye = (s == t).astype(F32)
    head_masks = [((lane >> 6) == hh).astype(F32).astype(BF16) for hh in range(HEADS_PER_GROUP)]
    t2 = lax.broadcasted_iota(jnp.int32, (L, L), 0)
    s2 = lax.broadcasted_iota(jnp.int32, (L, L), 1)
    tri_f = (s2 <= t2).astype(F32).astype(BF16)
    tri_b = (s2 >= t2).astype(F32).astype(BF16)

    def sp(x, full=True):
        hi = x.astype(BF16)
        return hi, ((x - hi.astype(F32)).astype(BF16) if full else None)

    def bd1(q):
        return jnp.concatenate([q * m for m in head_masks], axis=0)

    def bd(q, full=True):
        qh, ql = sp(q, full)
        return bd1(qh), (None if ql is None else bd1(ql))

    def bdmm(parts, w, nt=False):
        wh, wl = w
        three = [p[1] is not None for p in parts]
        rows = [p[0] for p in parts] + [p[1] for p, f in zip(parts, three) if f]
        main = _dg(jnp.concatenate(rows, axis=0) if len(rows) > 1 else rows[0], wh, nt)
        cross_rows = [p[0] for p, f in zip(parts, three) if f] if wl is not None else []
        cross = _dg(jnp.concatenate(cross_rows, axis=0) if len(cross_rows) > 1 else cross_rows[0], wl, nt) if cross_rows else None
        outs, off = [], 0
        for p in parts:
            m = p[0].shape[0]
            outs.append(main[off:off + m])
            off += m
        lo_off, cr_off = off, 0
        for i, (p, f) in enumerate(zip(parts, three)):
            if not f:
                continue
            m = p[0].shape[0]
            outs[i] = outs[i] + main[lo_off:lo_off + m]
            lo_off += m
            if cross is not None:
                outs[i] = outs[i] + cross[cr_off:cr_off + m]
                cr_off += m
        return outs

    def each(fn, *cols):
        return [fn(*args) for args in zip(*cols)]

    r, kk, v, lw, kd, b, h0, bwd = (list(col) for col in zip(*streams))
    strict = [(s > t) if bw else (s < t) for bw in bwd]
    incl = [(s >= t) if bw else (s <= t) for bw in bwd]

    def cumsum(lw_, bw):
        tri = tri_b if bw else tri_f
        l1 = lw_.astype(BF16)
        rem = lw_ - l1.astype(F32)
        l2 = rem.astype(BF16)
        l3 = (rem - l2.astype(F32)).astype(BF16)
        return _dg(tri, l1) + _dg(tri, l2) + _dg(tri, l3)

    c = each(cumsum, lw, bwd)
    c_last = each(lambda c_, bw: c_[0:1, :] if bw else c_[L - 1:L, :], c, bwd)
    e_neg = each(lambda c_: jnp.exp(-c_), c)
    kt = each(lambda kk_, c_, lw_: kk_ * jnp.exp(c_ - lw_), kk, c, lw)
    rt = each(lambda r_, c_: r_ * jnp.exp(c_), r, c)
    kdt = each(lambda kd_, e_: kd_ * e_, kd, e_neg)
    bt = each(lambda b_, e_: b_ * e_, b, e_neg)
    eye_g = each(lambda cl: eye * jnp.exp(cl), c_last)
    lhs1 = each(lambda a1, a2, a3: [sp(a1, False), sp(a2, False), sp(a3, False)], kt, rt, eye_g)
    gk = each(lambda p, q: bdmm(p, bd(q, False), nt=True), lhs1, kdt)
    gb = each(lambda p, q: bdmm(p, bd(q, False), nt=True), lhs1, bt)
    a_k = each(lambda m, g_: jnp.where(m, g_[0], 0.0), strict, gk)
    a_rk = each(lambda m, g_: jnp.where(m, g_[1], 0.0), incl, gk)
    a_b = each(lambda m, g_: jnp.where(m, g_[0], 0.0), strict, gb)
    a_rb = each(lambda m, g_: jnp.where(m, g_[1], 0.0), incl, gb)

    mm = lambda p, q: bdmm([sp(p, False)], bd(q, False))[0]
    t_inv = each(lambda a: eye - jnp.where((t >> 1) == (s >> 1), a, 0.0), a_b)
    m = 2
    while m < L:
        shift = m.bit_length() - 1
        join = ((t >> (shift + 1)) == (s >> (shift + 1))) & ((t >> shift) != (s >> shift))
        p_m = each(lambda ti, a: mm(ti, jnp.where(join, a, 0.0)), t_inv, a_b)
        t_inv = each(lambda ti, p_: ti - mm(p_, ti), t_inv, p_m)
        m *= 2

    gh = each(lambda p, eg, h_: bdmm(p[:2] + [sp(eg)], bd(h_)), lhs1, eye_g, h0)
    gv = each(lambda ak, ark, g_, v_: bdmm([sp(ak, False), sp(ark, False), sp(g_[2], False)], bd(v_, False)),
              a_k, a_rk, gk, v)
    u = each(lambda ti, gh_, gv_: bdmm([sp(ti, False)], bd(gh_[0] + gv_[0], False))[0], t_inv, gh, gv)
    gu = each(lambda arb, g_, u_: bdmm([sp(arb, False), sp(g_[2], False)], bd(u_, False)), a_rb, gb, u)
    y = each(lambda gh_, gv_, gu_: gh_[1] + gv_[1] - gu_[0], gh, gv, gu)
    h_next = each(lambda gh_, gv_, gu_: gh_[2] + gv_[2] - gu_[1], gh, gv, gu)
    return list(zip(y, h_next))


def _wkv_kernel(rf, kkf, vf, lwf, kdf, bf, rb, kkb, vb, lwb, kdb, bb, s0_ref, yf_ref, yb_ref, st_ref, h_ref):
    c = pl.program_id(1)

    @pl.when(c == 0)
    def _():
        h_ref[...] = s0_ref[...]

    dirs = ((rf, kkf, vf, lwf, kdf, bf, yf_ref), (rb, kkb, vb, lwb, kdb, bb, yb_ref))
    groups = [slice(g * GROUP, (g + 1) * GROUP) for g in range(N_GROUPS)]
    keys = [(i, d, sl) for i in range(WKV_BATCH) for d in range(2) for sl in groups]
    streams = [tuple(ref[i, :, sl] for ref in dirs[d][:6]) + (h_ref[i, d, :, sl], d == 1) for i, d, sl in keys]
    for (i, d, sl), (y, hn) in zip(keys, _wkv_streams(streams)):
        dirs[d][6][i, :, sl] = y
        h_ref[i, d, :, sl] = hn

    @pl.when(c == pl.num_programs(1) - 1)
    def _():
        st_ref[...] = h_ref[...]


def _wkv(r, kk, v, lw0, lw1, kd0, kd1, b0, b1, s0):
    B, T, _ = r.shape
    nc = T // CHUNK
    nb = WKV_BATCH
    assert B % nb == 0
    fwd = pl.BlockSpec((nb, CHUNK, A_WIDTH), lambda b, c: (b, c, 0))
    bwd = pl.BlockSpec((nb, CHUNK, A_WIDTH), lambda b, c: (b, nc - 1 - c, 0))
    st = pl.BlockSpec((nb, 2, HEAD_DIM, A_WIDTH), lambda b, c: (b, 0, 0, 0))
    return pl.pallas_call(
        _wkv_kernel,
        grid=(B // nb, nc),
        in_specs=[fwd] * 6 + [bwd] * 6 + [st],
        out_specs=[fwd, bwd, st],
        out_shape=[jax.ShapeDtypeStruct((B, T, A_WIDTH), F32), jax.ShapeDtypeStruct((B, T, A_WIDTH), F32),
                   jax.ShapeDtypeStruct((B, 2, HEAD_DIM, A_WIDTH), F32)],
        scratch_shapes=[pltpu.VMEM((nb, 2, HEAD_DIM, A_WIDTH), F32)],
        compiler_params=pltpu.CompilerParams(dimension_semantics=("parallel", "arbitrary"),
                                             vmem_limit_bytes=VMEM_LIMIT),
        name="wkv",
    )(r, kk, v, lw0, kd0, b0, r, kk, v, lw1, kd1, b1, s0)


def _post_kernel(*refs, latent):
    if latent:
        (x_ref, mod_ref, yf_ref, yb_ref, bv_ref, ga_ref, cu_ref, cup_ref, cun_ref, bg_ref,
         gnw_ref, gnb_ref, cw_ref, wout_ref, fg_ref, ones_ref, o_ref) = refs
    else:
        (x_ref, mod_ref, yf_ref, yb_ref, bv_ref, ga_ref, cu_ref, bg_ref,
         gnw_ref, gnb_ref, cw_ref, wout_ref, fg_ref, ones_ref, o_ref) = refs
    x = x_ref[0]
    gate = mod_ref[0][:, 2 * D_MODEL:]
    y = yf_ref[0] + yb_ref[0]
    inv_n = 1.0 / HEAD_DIM
    mean = _segsum(y, ones_ref) * inv_n
    dlt = y - mean
    var = _segsum(dlt * dlt, ones_ref) * inv_n
    yn = dlt * lax.rsqrt(var + GN_EPS) * gnw_ref[...] + gnb_ref[...] + bv_ref[0]
    ya = yn * ga_ref[0]

    cu = cu_ref[0]
    w_prev, w_mid, w_next = cw_ref[0:1, :], cw_ref[1:2, :], cw_ref[2:3, :]
    if latent:
        half = B_WIDTH // 2
        t = pl.program_id(1)
        ph, nh = _shift_rows(cu[:, :half], GRID_W)
        top = jnp.where(t == 0, 0.0, cup_ref[0][:, half:])
        bot = jnp.where(t == pl.num_programs(1) - 1, 0.0, cun_ref[0][:, half:])
        pv = jnp.concatenate([top, cu[:TOKEN_TILE - HALO, half:]], axis=0)
        nv = jnp.concatenate([cu[HALO:, half:], bot], axis=0)
        prev = jnp.concatenate([ph, pv], axis=1)
        nxt = jnp.concatenate([nh, nv], axis=1)
    else:
        prev, nxt = _shift_rows(cu, TOKEN_TILE)
    yb = bg_ref[0] * (w_prev * prev + w_mid * cu + w_next * nxt)

    u = _dg(ya.astype(BF16), wout_ref[:A_WIDTH, :]) + _dg(yb.astype(BF16), wout_ref[A_WIDTH:, :])
    o = x + gate * u
    o_ref[0] = o * lax.rsqrt(jnp.mean(o * o, axis=-1, keepdims=True) + NORM_EPS) * fg_ref[...]


def _post(x, mod, yf, yb, bv, ga, cu, bg, weights, latent):
    B, T, _ = x.shape
    nt = T // TOKEN_TILE
    tok = lambda w: pl.BlockSpec((1, TOKEN_TILE, w), lambda b, t: (b, t, 0))
    mod_spec = pl.BlockSpec((1, 1, 3 * D_MODEL), (lambda b, t: (b, 0, 0)) if mod.shape[0] > 1 else (lambda b, t: (0, 0, 0)))
    per_tile = TOKEN_TILE // HALO
    n_halo = T // HALO
    halo_prev = pl.BlockSpec((1, HALO, B_WIDTH), lambda b, t: (b, jnp.maximum(t * per_tile - 1, 0), 0))
    halo_next = pl.BlockSpec((1, HALO, B_WIDTH), lambda b, t: (b, jnp.minimum((t + 1) * per_tile, n_halo - 1), 0))
    acts = [x, mod, yf, yb, bv, ga, cu] + ([cu, cu] if latent else []) + [bg]
    specs = [tok(D_MODEL), mod_spec] + [tok(A_WIDTH)] * 5 + ([halo_prev, halo_next] if latent else []) + [tok(B_WIDTH)]
    return pl.pallas_call(
        functools.partial(_post_kernel, latent=latent),
        grid=(B, nt),
        in_specs=specs + [_const_spec(w.shape) for w in weights],
        out_specs=tok(D_MODEL),
        out_shape=jax.ShapeDtypeStruct((B, T, D_MODEL), F32),
        compiler_params=pltpu.CompilerParams(dimension_semantics=("parallel", "parallel"),
                                             vmem_limit_bytes=VMEM_LIMIT),
        name="post",
    )(*acts, *weights)


def _block_diag2(w):
    z = jnp.zeros_like(w[0])
    return jnp.concatenate([jnp.concatenate([w[0], z], axis=1), jnp.concatenate([z, w[1]], axis=1)], axis=0)


def kernel(x_prompt, x_sample, c, state_wkv, c_ctx, w_ada, b_ada, norm_g, w_in, shift_mu, decay_w0, decay_down,
           decay_up, iclr_bias, iclr_down, iclr_up, kk_scale, ka_scale, bonus_rk, gn_w, gn_b, conv_w, w_out, final_g):
    assert w_in.shape[0] == 1, "single layer"
    n_lat = c.shape[0]
    row = lambda z: z.reshape(1, -1)

    pad = (-(n_lat + 1)) % 8
    cvec = jnp.concatenate([c, c_ctx[None, :], jnp.zeros((pad, D_MODEL), F32)], axis=0)
    mod = _adaln(cvec, w_ada[0].astype(BF16), row(b_ada[0]))
    mod_lat = mod[:n_lat, None, :]
    mod_ctx = mod[n_lat:n_lat + 1, None, :]

    seg_id = np.arange(GROUP) // HEAD_DIM
    ones_bd = jnp.asarray((seg_id[:, None] == seg_id[None, :]).astype(np.float32), dtype=BF16)
    w_down = jnp.concatenate([decay_down[0, 0], decay_down[0, 1], iclr_down[0, 0], iclr_down[0, 1]], axis=1)
    pre_w = [row(norm_g[0]), w_in[0].astype(BF16), w_down.astype(BF16), _block_diag2(decay_up[0]).astype(BF16),
             _block_diag2(iclr_up[0]).astype(BF16), row(decay_w0[0]), row(iclr_bias[0]), shift_mu[0],
             row(kk_scale[0]), row(ka_scale[0]), row(bonus_rk[0]), ones_bd]
    post_w = [row(gn_w[0]), row(gn_b[0]), conv_w[0], w_out[0].astype(BF16), row(final_g), ones_bd]

    def to_state(s):
        return jnp.transpose(s, (0, 1, 4, 2, 3)).reshape(s.shape[0], 2, HEAD_DIM, A_WIDTH)

    def from_state(s):
        return jnp.transpose(s.reshape(s.shape[0], 2, HEAD_DIM, A_HEADS, HEAD_DIM), (0, 1, 3, 4, 2))

    def layer(x, mod_x, s0, latent):
        seg = GRID_W if latent else x.shape[1]
        r, v, kk, lw0, lw1, kd0, kd1, b0, b1, bv, ga, cu, bg = _pre(x, mod_x, pre_w, seg)
        yf, yb, st = _wkv(r, kk, v, lw0, lw1, kd0, kd1, b0, b1, s0)
        return _post(x, mod_x, yf, yb, bv, ga, cu, bg, post_w, latent), st

    assert x_prompt.shape[1] == TOKEN_TILE, "context sequences are one token tile"
    s_zero = jnp.zeros((x_prompt.shape[0], 2, HEAD_DIM, A_WIDTH), F32)
    y_prompt, s_ctx = layer(x_prompt, mod_ctx, s_zero, False)
    y_sample, _ = layer(x_sample, mod_lat, to_state(state_wkv[:, 0]), True)
    new_state = from_state(s_ctx)[:, None]
    return (y_prompt, y_sample, new_state)
```

```python
import functools
import math

import numpy as np
import jax
import jax.numpy as jnp
from jax import lax
from jax.experimental import pallas as pl
from jax.experimental.pallas import tpu as pltpu

D_MODEL = 1024
A_WIDTH = 512
B_WIDTH = 512
HEAD_DIM = 64
A_HEADS = 8
GRID_W = 64
LORA = 64
IN_WIDTH = 4 * A_WIDTH + 4 * B_WIDTH
NORM_EPS = 1e-6
GN_EPS = 64e-5

CHUNK = 64
GROUP = 256
HEADS_PER_GROUP = GROUP // HEAD_DIM
N_GROUPS = A_WIDTH // GROUP
WKV_BATCH = 4
TOKEN_TILE = 512
HALO = GRID_W
VMEM_LIMIT = 56 * 1024 * 1024

F32 = jnp.float32
BF16 = jnp.bfloat16


def _dg(a, b, nt=False):
    dims = (((1,), (1 if nt else 0,)), ((), ()))
    return lax.dot_general(a, b, dims, preferred_element_type=F32)


def _segsum(z, ones_ref):
    zb = z.astype(BF16)
    return jnp.concatenate([_dg(zb[:, g * GROUP:(g + 1) * GROUP], ones_ref[...]) for g in range(N_GROUPS)], axis=1)


def _adaln_kernel(c_ref, w_ref, b_ref, o_ref):
    c = c_ref[...]
    s = c * jax.nn.sigmoid(c)
    o_ref[...] = _dg(s.astype(BF16), w_ref[...]) + b_ref[...]


def _adaln(cvec, w_ada_bf, b_ada):
    rows = cvec.shape[0]
    return pl.pallas_call(
        _adaln_kernel,
        out_shape=jax.ShapeDtypeStruct((rows, 3 * D_MODEL), F32),
        compiler_params=pltpu.CompilerParams(vmem_limit_bytes=VMEM_LIMIT),
        name="adaln",
    )(cvec, w_ada_bf, b_ada)


def _shift_rows(p, seg):
    rows = p.shape[0]
    assert seg & (seg - 1) == 0
    pos = lax.broadcasted_iota(jnp.int32, p.shape, 0) & (seg - 1)
    prev = jnp.where(pos == 0, 0.0, pltpu.roll(p, 1, 0))
    nxt = jnp.where(pos == seg - 1, 0.0, pltpu.roll(p, rows - 1, 0))
    return prev, nxt


def _pre_kernel(x_ref, mod_ref, g_ref, win_ref, wdn_ref, wdu_ref, wiu_ref, w0_ref, ab_ref, mu_ref,
                kks_ref, kas_ref, brk_ref, ones_ref,
                r_o, v_o, kk_o, lw0_o, lw1_o, kd0_o, kd1_o, b0_o, b1_o, bv_o, ga_o, cu_o, bg_o, *, seg):
    x = x_ref[0]
    mod = mod_ref[0]
    shift = mod[:, :D_MODEL]
    scale = mod[:, D_MODEL:2 * D_MODEL]
    xn = x * lax.rsqrt(jnp.mean(x * x, axis=-1, keepdims=True) + NORM_EPS) * g_ref[...]
    h = xn * (1.0 + scale) + shift
    hb = h.astype(BF16)

    def proj(i):
        return _dg(hb, win_ref[:, i * A_WIDTH:(i + 1) * A_WIDTH])

    def mixed(i):
        p = proj(i)
        prev, nxt = _shift_rows(p, seg)
        return p + mu_ref[i:i + 1, :] * (0.5 * (prev + nxt) - p)

    r = mixed(0)
    k = mixed(1)
    v = mixed(2)
    r_o[0] = r
    v_o[0] = v

    low = _dg(hb, wdn_ref[...])
    lwz = jnp.tanh(low[:, :2 * LORA]).astype(BF16)
    la = low[:, 2 * LORA:].astype(BF16)
    wpre = _dg(lwz, wdu_ref[...]) + w0_ref[...]
    apre = _dg(la, wiu_ref[...]) + ab_ref[...]
    lw = -math.exp(-0.5) * jax.nn.sigmoid(wpre)
    a = jax.nn.sigmoid(apre)
    lw0_o[0] = lw[:, :A_WIDTH]
    lw1_o[0] = lw[:, A_WIDTH:]

    kks = k * kks_ref[...]
    ss = _segsum(kks * kks, ones_ref)
    kk = kks * lax.rsqrt(jnp.maximum(ss, 1e-24))
    kk_o[0] = kk
    kas = kas_ref[...]
    a0 = a[:, :A_WIDTH]
    a1 = a[:, A_WIDTH:]
    kd0 = k * (1.0 + (a0 - 1.0) * kas)
    kd1 = k * (1.0 + (a1 - 1.0) * kas)
    kd0_o[0] = kd0
    kd1_o[0] = kd1
    b0_o[0] = kk * a0
    b1_o[0] = kk * a1
    bonus = _segsum(r * (kd0 + kd1) * brk_ref[...], ones_ref)
    bv_o[0] = (bonus * v).astype(BF16)

    ga = proj(3)
    ga_o[0] = (ga * jax.nn.sigmoid(ga)).astype(BF16)
    bgate = proj(4)
    cgate = proj(5)
    uconv = proj(6)
    gb = proj(7)
    cu_o[0] = (cgate * uconv).astype(BF16)
    bg_o[0] = (bgate * (gb * jax.nn.sigmoid(gb))).astype(BF16)


def _const_spec(shape):
    nd = len(shape)
    return pl.BlockSpec(shape, lambda *_: (0,) * nd, pipeline_mode=pl.Buffered(1))


def _pre(x, mod, weights, seg):
    B, T, _ = x.shape
    tile = min(TOKEN_TILE, T)
    nt = T // tile
    tok = lambda w: pl.BlockSpec((1, tile, w), lambda b, t: (b, t, 0))
    mod_spec = pl.BlockSpec((1, 1, 3 * D_MODEL), (lambda b, t: (b, 0, 0)) if mod.shape[0] > 1 else (lambda b, t: (0, 0, 0)))
    n_f32, n_bf16 = 9, 4
    return pl.pallas_call(
        functools.partial(_pre_kernel, seg=seg),
        grid=(B, nt),
        in_specs=[tok(D_MODEL), mod_spec] + [_const_spec(w.shape) for w in weights],
        out_specs=[tok(A_WIDTH)] * (n_f32 + n_bf16),
        out_shape=[jax.ShapeDtypeStruct((B, T, A_WIDTH), F32)] * n_f32
        + [jax.ShapeDtypeStruct((B, T, A_WIDTH), BF16)] * n_bf16,
        compiler_params=pltpu.CompilerParams(dimension_semantics=("parallel", "parallel"),
                                             vmem_limit_bytes=VMEM_LIMIT),
        name="pre",
    )(x, mod, *weights)


def _wkv_streams(streams):
    L = CHUNK
    t = lax.broadcasted_iota(jnp.int32, (L, GROUP), 0)
    lane = lax.broadcasted_iota(jnp.int32, (L, GROUP), 1)
    s = lane & (L - 1)
    eye = (s == t).astype(F32)
    head_masks = [((lane >> 6) == hh).astype(F32).astype(BF16) for hh in range(HEADS_PER_GROUP)]
    t2 = lax.broadcasted_iota(jnp.int32, (L, L), 0)
    s2 = lax.broadcasted_iota(jnp.int32, (L, L), 1)
    tri_f = (s2 <= t2).astype(F32).astype(BF16)
    tri_b = (s2 >= t2).astype(F32).astype(BF16)

    def sp(x, full=True):
        hi = x.astype(BF16)
        return hi, ((x - hi.astype(F32)).astype(BF16) if full else None)

    def bd1(q):
        return jnp.concatenate([q * m for m in head_masks], axis=0)

    def bd(q, full=True):
        qh, ql = sp(q, full)
        return bd1(qh), (None if ql is None else bd1(ql))

    def bdmm(parts, w, nt=False):
        wh, wl = w
        three = [p[1] is not None for p in parts]
        rows = [p[0] for p in parts] + [p[1] for p, f in zip(parts, three) if f]
        main = _dg(jnp.concatenate(rows, axis=0) if len(rows) > 1 else rows[0], wh, nt)
        cross_rows = [p[0] for p, f in zip(parts, three) if f] if wl is not None else []
        cross = _dg(jnp.concatenate(cross_rows, axis=0) if len(cross_rows) > 1 else cross_rows[0], wl, nt) if cross_rows else None
        outs, off = [], 0
        for p in parts:
            m = p[0].shape[0]
            outs.append(main[off:off + m])
            off += m
        lo_off, cr_off = off, 0
        for i, (p, f) in enumerate(zip(parts, three)):
            if not f:
                continue
            m = p[0].shape[0]
            outs[i] = outs[i] + main[lo_off:lo_off + m]
            lo_off += m
            if cross is not None:
                outs[i] = outs[i] + cross[cr_off:cr_off + m]
                cr_off += m
        return outs

    def each(fn, *cols):
        return [fn(*args) for args in zip(*cols)]

    r, kk, v, lw, kd, b, h0, bwd = (list(col) for col in zip(*streams))
    strict = [(s > t) if bw else (s < t) for bw in bwd]
    incl = [(s >= t) if bw else (s <= t) for bw in bwd]

    def cumsum(lw_, bw):
        tri = tri_b if bw else tri_f
        l1 = lw_.astype(BF16)
        rem = lw_ - l1.astype(F32)
        l2 = rem.astype(BF16)
        l3 = (rem - l2.astype(F32)).astype(BF16)
        return _dg(tri, l1) + _dg(tri, l2) + _dg(tri, l3)

    c = each(cumsum, lw, bwd)
    c_last = each(lambda c_, bw: c_[0:1, :] if bw else c_[L - 1:L, :], c, bwd)
    e_neg = each(lambda c_: jnp.exp(-c_), c)
    kt = each(lambda kk_, c_, lw_: kk_ * jnp.exp(c_ - lw_), kk, c, lw)
    rt = each(lambda r_, c_: r_ * jnp.exp(c_), r, c)
    kdt = each(lambda kd_, e_: kd_ * e_, kd, e_neg)
    bt = each(lambda b_, e_: b_ * e_, b, e_neg)
    eye_g = each(lambda cl: eye * jnp.exp(cl), c_last)
    lhs1 = each(lambda a1, a2, a3: [sp(a1, False), sp(a2, False), sp(a3, False)], kt, rt, eye_g)
    gk = each(lambda p, q: bdmm(p, bd(q, False), nt=True), lhs1, kdt)
    gb = each(lambda p, q: bdmm(p, bd(q, False), nt=True), lhs1, bt)
    a_k = each(lambda m, g_: jnp.where(m, g_[0], 0.0), strict, gk)
    a_rk = each(lambda m, g_: jnp.where(m, g_[1], 0.0), incl, gk)
    a_b = each(lambda m, g_: jnp.where(m, g_[0], 0.0), strict, gb)
    a_rb = each(lambda m, g_: jnp.where(m, g_[1], 0.0), incl, gb)

    mm = lambda p, q: bdmm([sp(p, False)], bd(q, False))[0]
    t_inv = each(lambda a: eye - jnp.where((t >> 1) == (s >> 1), a, 0.0), a_b)
    m = 2
    while m < L:
        shift = m.bit_length() - 1
        join = ((t >> (shift + 1)) == (s >> (shift + 1))) & ((t >> shift) != (s >> shift))
        p_m = each(lambda ti, a: mm(ti, jnp.where(join, a, 0.0)), t_inv, a_b)
        t_inv = each(lambda ti, p_: ti - mm(p_, ti), t_inv, p_m)
        m *= 2

    gh = each(lambda p, eg, h_: bdmm(p[:2] + [sp(eg)], bd(h_)), lhs1, eye_g, h0)
    gv = each(lambda ak, ark, g_, v_: bdmm([sp(ak, False), sp(ark, False), sp(g_[2], False)], bd(v_, False)),
              a_k, a_rk, gk, v)
    u = each(lambda ti, gh_, gv_: bdmm([sp(ti, False)], bd(gh_[0] + gv_[0], False))[0], t_inv, gh, gv)
    gu = each(lambda arb, g_, u_: bdmm([sp(arb, False), sp(g_[2], False)], bd(u_, False)), a_rb, gb, u)
    y = each(lambda gh_, gv_, gu_: gh_[1] + gv_[1] - gu_[0], gh, gv, gu)
    h_next = each(lambda gh_, gv_, gu_: gh_[2] + gv_[2] - gu_[1], gh, gv, gu)
    return list(zip(y, h_next))


def _wkv_kernel(rf, kkf, vf, lwf, kdf, bf, rb, kkb, vb, lwb, kdb, bb, s0_ref, yf_ref, yb_ref, st_ref, h_ref):
    c = pl.program_id(1)

    @pl.when(c == 0)
    def _():
        h_ref[...] = s0_ref[...]

    dirs = ((rf, kkf, vf, lwf, kdf, bf, yf_ref), (rb, kkb, vb, lwb, kdb, bb, yb_ref))
    groups = [slice(g * GROUP, (g + 1) * GROUP) for g in range(N_GROUPS)]
    keys = [(i, d, sl) for i in range(WKV_BATCH) for d in range(2) for sl in groups]
    streams = [tuple(ref[i, :, sl] for ref in dirs[d][:6]) + (h_ref[i, d, :, sl], d == 1) for i, d, sl in keys]
    for (i, d, sl), (y, hn) in zip(keys, _wkv_streams(streams)):
        dirs[d][6][i, :, sl] = y
        h_ref[i, d, :, sl] = hn

    @pl.when(c == pl.num_programs(1) - 1)
    def _():
        st_ref[...] = h_ref[...]


def _wkv(r, kk, v, lw0, lw1, kd0, kd1, b0, b1, s0):
    B, T, _ = r.shape
    nc = T // CHUNK
    nb = WKV_BATCH
    assert B % nb == 0
    fwd = pl.BlockSpec((nb, CHUNK, A_WIDTH), lambda b, c: (b, c, 0))
    bwd = pl.BlockSpec((nb, CHUNK, A_WIDTH), lambda b, c: (b, nc - 1 - c, 0))
    st = pl.BlockSpec((nb, 2, HEAD_DIM, A_WIDTH), lambda b, c: (b, 0, 0, 0))
    return pl.pallas_call(
        _wkv_kernel,
        grid=(B // nb, nc),
        in_specs=[fwd] * 6 + [bwd] * 6 + [st],
        out_specs=[fwd, bwd, st],
        out_shape=[jax.ShapeDtypeStruct((B, T, A_WIDTH), F32), jax.ShapeDtypeStruct((B, T, A_WIDTH), F32),
                   jax.ShapeDtypeStruct((B, 2, HEAD_DIM, A_WIDTH), F32)],
        scratch_shapes=[pltpu.VMEM((nb, 2, HEAD_DIM, A_WIDTH), F32)],
        compiler_params=pltpu.CompilerParams(dimension_semantics=("parallel", "arbitrary"),
                                             vmem_limit_bytes=VMEM_LIMIT),
        name="wkv",
    )(r, kk, v, lw0, kd0, b0, r, kk, v, lw1, kd1, b1, s0)


def _post_kernel(*refs, latent, seq_len):
    if latent:
        (x_ref, mod_ref, yf_ref, yb_ref, bv_ref, ga_ref, cu_ref, cup_ref, cun_ref, bg_ref,
         gnw_ref, gnb_ref, cw_ref, wout_ref, fg_ref, ones_ref, o_ref) = refs
    else:
        (x_ref, mod_ref, yf_ref, yb_ref, bv_ref, ga_ref, cu_ref, bg_ref,
         gnw_ref, gnb_ref, cw_ref, wout_ref, fg_ref, ones_ref, o_ref) = refs
    x = x_ref[0]
    gate = mod_ref[0][:, 2 * D_MODEL:]
    y = yf_ref[0] + yb_ref[0]
    inv_n = 1.0 / HEAD_DIM
    mean = _segsum(y, ones_ref) * inv_n
    dlt = y - mean
    var = _segsum(dlt * dlt, ones_ref) * inv_n
    yn = dlt * lax.rsqrt(var + GN_EPS) * gnw_ref[...] + gnb_ref[...] + bv_ref[0].astype(F32)
    ya = yn * ga_ref[0].astype(F32)

    cu = cu_ref[0].astype(F32)
    w_prev, w_mid, w_next = cw_ref[0:1, :], cw_ref[1:2, :], cw_ref[2:3, :]
    if latent:
        half = B_WIDTH // 2
        t = pl.program_id(1)
        ph, nh = _shift_rows(cu[:, :half], GRID_W)
        top = jnp.where(t == 0, 0.0, cup_ref[0][:, half:].astype(F32))
        bot = jnp.where(t == pl.num_programs(1) - 1, 0.0, cun_ref[0][:, half:].astype(F32))
        pv = jnp.concatenate([top, cu[:cu.shape[0] - HALO, half:]], axis=0)
        nv = jnp.concatenate([cu[HALO:, half:], bot], axis=0)
        prev = jnp.concatenate([ph, pv], axis=1)
        nxt = jnp.concatenate([nh, nv], axis=1)
    else:
        prev, nxt = _shift_rows(cu, seq_len)
    yb = bg_ref[0].astype(F32) * (w_prev * prev + w_mid * cu + w_next * nxt)

    u = _dg(ya.astype(BF16), wout_ref[:A_WIDTH, :]) + _dg(yb.astype(BF16), wout_ref[A_WIDTH:, :])
    o = x + gate * u
    o_ref[0] = o * lax.rsqrt(jnp.mean(o * o, axis=-1, keepdims=True) + NORM_EPS) * fg_ref[...]


def _post(x, mod, yf, yb, bv, ga, cu, bg, weights, latent):
    B, T, _ = x.shape
    tile = min(TOKEN_TILE, T)
    nt = T // tile
    tok = lambda w: pl.BlockSpec((1, tile, w), lambda b, t: (b, t, 0))
    mod_spec = pl.BlockSpec((1, 1, 3 * D_MODEL), (lambda b, t: (b, 0, 0)) if mod.shape[0] > 1 else (lambda b, t: (0, 0, 0)))
    per_tile = tile // HALO
    n_halo = T // HALO
    halo_prev = pl.BlockSpec((1, HALO, B_WIDTH), lambda b, t: (b, jnp.maximum(t * per_tile - 1, 0), 0))
    halo_next = pl.BlockSpec((1, HALO, B_WIDTH), lambda b, t: (b, jnp.minimum((t + 1) * per_tile, n_halo - 1), 0))
    acts = [x, mod, yf, yb, bv, ga, cu] + ([cu, cu] if latent else []) + [bg]
    specs = [tok(D_MODEL), mod_spec] + [tok(A_WIDTH)] * 5 + ([halo_prev, halo_next] if latent else []) + [tok(B_WIDTH)]
    return pl.pallas_call(
        functools.partial(_post_kernel, latent=latent, seq_len=T),
        grid=(B, nt),
        in_specs=specs + [_const_spec(w.shape) for w in weights],
        out_specs=tok(D_MODEL),
        out_shape=jax.ShapeDtypeStruct((B, T, D_MODEL), F32),
        compiler_params=pltpu.CompilerParams(dimension_semantics=("parallel", "parallel"),
                                             vmem_limit_bytes=VMEM_LIMIT),
        name="post",
    )(*acts, *weights)


def _block_diag2(w):
    z = jnp.zeros_like(w[0])
    return jnp.concatenate([jnp.concatenate([w[0], z], axis=1), jnp.concatenate([z, w[1]], axis=1)], axis=0)


def kernel(x_prompt, x_sample, c, state_wkv, c_ctx, w_ada, b_ada, norm_g, w_in, shift_mu, decay_w0, decay_down,
           decay_up, iclr_bias, iclr_down, iclr_up, kk_scale, ka_scale, bonus_rk, gn_w, gn_b, conv_w, w_out, final_g):
    assert w_in.shape[0] == 1, "single layer"
    n_lat = c.shape[0]
    row = lambda z: z.reshape(1, -1)

    pad = (-(n_lat + 1)) % 8
    cvec = jnp.concatenate([c, c_ctx[None, :], jnp.zeros((pad, D_MODEL), F32)], axis=0)
    mod = _adaln(cvec, w_ada[0].astype(BF16), row(b_ada[0]))
    mod_lat = mod[:n_lat, None, :]
    mod_ctx = mod[n_lat:n_lat + 1, None, :]

    seg_id = np.arange(GROUP) // HEAD_DIM
    ones_bd = jnp.asarray((seg_id[:, None] == seg_id[None, :]).astype(np.float32), dtype=BF16)
    w_down = jnp.concatenate([decay_down[0, 0], decay_down[0, 1], iclr_down[0, 0], iclr_down[0, 1]], axis=1)
    pre_w = [row(norm_g[0]), w_in[0].astype(BF16), w_down.astype(BF16), _block_diag2(decay_up[0]).astype(BF16),
             _block_diag2(iclr_up[0]).astype(BF16), row(decay_w0[0]), row(iclr_bias[0]), shift_mu[0],
             row(kk_scale[0]), row(ka_scale[0]), row(bonus_rk[0]), ones_bd]
    post_w = [row(gn_w[0]), row(gn_b[0]), conv_w[0], w_out[0].astype(BF16), row(final_g), ones_bd]

    def to_state(s):
        return jnp.transpose(s, (0, 1, 4, 2, 3)).reshape(s.shape[0], 2, HEAD_DIM, A_WIDTH)

    def from_state(s):
        return jnp.transpose(s.reshape(s.shape[0], 2, HEAD_DIM, A_HEADS, HEAD_DIM), (0, 1, 3, 4, 2))

    def layer(x, mod_x, s0, latent):
        seg = GRID_W if latent else x.shape[1]
        r, v, kk, lw0, lw1, kd0, kd1, b0, b1, bv, ga, cu, bg = _pre(x, mod_x, pre_w, seg)
        yf, yb, st = _wkv(r, kk, v, lw0, lw1, kd0, kd1, b0, b1, s0)
        return _post(x, mod_x, yf, yb, bv, ga, cu, bg, post_w, latent), st

    assert x_prompt.shape[1] <= TOKEN_TILE, "a context sequence must fit one token tile (its conv has no halo)"
    s_zero = jnp.zeros((x_prompt.shape[0], 2, HEAD_DIM, A_WIDTH), F32)
    y_prompt, s_ctx = layer(x_prompt, mod_ctx, s_zero, False)
    y_sample, _ = layer(x_sample, mod_lat, to_state(state_wkv[:, 0]), True)
    new_state = from_state(s_ctx)[:, None]
    return (y_prompt, y_sample, new_state)
```

```python
import functools
import math

import numpy as np
import jax
import jax.numpy as jnp
from jax import lax
from jax.experimental import pallas as pl
from jax.experimental.pallas import tpu as pltpu

D_MODEL = 1024
A_WIDTH = 512
B_WIDTH = 512
HEAD_DIM = 64
A_HEADS = 8
GRID_W = 64
LORA = 64
IN_WIDTH = 4 * A_WIDTH + 4 * B_WIDTH
NORM_EPS = 1e-6
GN_EPS = 64e-5

CHUNK = 64
GROUP = 256
HEADS_PER_GROUP = GROUP // HEAD_DIM
N_GROUPS = A_WIDTH // GROUP
WKV_BATCH = 4
TOKEN_TILE = 512
HALO = GRID_W
VMEM_LIMIT = 56 * 1024 * 1024

F32 = jnp.float32
BF16 = jnp.bfloat16


def _dg(a, b, nt=False):
    dims = (((1,), (1 if nt else 0,)), ((), ()))
    return lax.dot_general(a, b, dims, preferred_element_type=F32)


def _segsum(z, ones_ref):
    zb = z.astype(BF16)
    return jnp.concatenate([_dg(zb[:, g * GROUP:(g + 1) * GROUP], ones_ref[...]) for g in range(N_GROUPS)], axis=1)


def _adaln_kernel(c_ref, w_ref, b_ref, o_ref):
    c = c_ref[...]
    s = c * jax.nn.sigmoid(c)
    o_ref[...] = _dg(s.astype(BF16), w_ref[...]) + b_ref[...]


def _adaln(cvec, w_ada_bf, b_ada):
    rows = cvec.shape[0]
    return pl.pallas_call(
        _adaln_kernel,
        out_shape=jax.ShapeDtypeStruct((rows, 3 * D_MODEL), F32),
        compiler_params=pltpu.CompilerParams(vmem_limit_bytes=VMEM_LIMIT),
        name="adaln",
    )(cvec, w_ada_bf, b_ada)


def _shift_rows(p, seg):
    rows = p.shape[0]
    assert seg & (seg - 1) == 0
    pos = lax.broadcasted_iota(jnp.int32, p.shape, 0) & (seg - 1)
    prev = jnp.where(pos == 0, 0.0, pltpu.roll(p, 1, 0))
    nxt = jnp.where(pos == seg - 1, 0.0, pltpu.roll(p, rows - 1, 0))
    return prev, nxt


def _pre_kernel(x_ref, mod_ref, g_ref, win_ref, wdn_ref, wdu_ref, wiu_ref, w0_ref, ab_ref, mu_ref,
                kks_ref, kas_ref, brk_ref, ones_ref,
                r_o, v_o, kk_o, lw0_o, lw1_o, kd0_o, kd1_o, b0_o, b1_o, bv_o, ga_o, cu_o, bg_o, *, seg):
    x = x_ref[0]
    mod = mod_ref[0]
    shift = mod[:, :D_MODEL]
    scale = mod[:, D_MODEL:2 * D_MODEL]
    xn = x * lax.rsqrt(jnp.mean(x * x, axis=-1, keepdims=True) + NORM_EPS) * g_ref[...]
    h = xn * (1.0 + scale) + shift
    hb = h.astype(BF16)

    def proj(i):
        return _dg(hb, win_ref[:, i * A_WIDTH:(i + 1) * A_WIDTH])

    def mixed(i):
        p = proj(i)
        prev, nxt = _shift_rows(p, seg)
        return p + mu_ref[i:i + 1, :] * (0.5 * (prev + nxt) - p)

    r = mixed(0)
    k = mixed(1)
    v = mixed(2)
    r_o[0] = r
    v_o[0] = v

    low = _dg(hb, wdn_ref[...])
    lwz = jnp.tanh(low[:, :2 * LORA]).astype(BF16)
    la = low[:, 2 * LORA:].astype(BF16)
    wpre = _dg(lwz, wdu_ref[...]) + w0_ref[...]
    apre = _dg(la, wiu_ref[...]) + ab_ref[...]
    lw = -math.exp(-0.5) * jax.nn.sigmoid(wpre)
    a = jax.nn.sigmoid(apre)
    lw0_o[0] = lw[:, :A_WIDTH]
    lw1_o[0] = lw[:, A_WIDTH:]

    kks = k * kks_ref[...]
    ss = _segsum(kks * kks, ones_ref)
    kk = kks * lax.rsqrt(jnp.maximum(ss, 1e-24))
    kk_o[0] = kk
    kas = kas_ref[...]
    a0 = a[:, :A_WIDTH]
    a1 = a[:, A_WIDTH:]
    kd0 = k * (1.0 + (a0 - 1.0) * kas)
    kd1 = k * (1.0 + (a1 - 1.0) * kas)
    kd0_o[0] = kd0
    kd1_o[0] = kd1
    b0_o[0] = kk * a0
    b1_o[0] = kk * a1
    bonus = _segsum(r * (kd0 + kd1) * brk_ref[...], ones_ref)
    bv_o[0] = (bonus * v).astype(BF16)

    ga = proj(3)
    ga_o[0] = (ga * jax.nn.sigmoid(ga)).astype(BF16)
    bgate = proj(4)
    cgate = proj(5)
    uconv = proj(6)
    gb = proj(7)
    cu_o[0] = (cgate * uconv).astype(BF16)
    bg_o[0] = (bgate * (gb * jax.nn.sigmoid(gb))).astype(BF16)


def _const_spec(shape):
    nd = len(shape)
    return pl.BlockSpec(shape, lambda *_: (0,) * nd, pipeline_mode=pl.Buffered(1))


def _pre(x, mod, weights, seg):
    B, T, _ = x.shape
    tile = min(TOKEN_TILE, T)
    nt = T // tile
    tok = lambda w: pl.BlockSpec((1, tile, w), lambda b, t: (b, t, 0))
    mod_spec = pl.BlockSpec((1, 1, 3 * D_MODEL), (lambda b, t: (b, 0, 0)) if mod.shape[0] > 1 else (lambda b, t: (0, 0, 0)))
    n_f32, n_bf16 = 9, 4
    return pl.pallas_call(
        functools.partial(_pre_kernel, seg=seg),
        grid=(B, nt),
        in_specs=[tok(D_MODEL), mod_spec] + [_const_spec(w.shape) for w in weights],
        out_specs=[tok(A_WIDTH)] * (n_f32 + n_bf16),
        out_shape=[jax.ShapeDtypeStruct((B, T, A_WIDTH), F32)] * n_f32
        + [jax.ShapeDtypeStruct((B, T, A_WIDTH), BF16)] * n_bf16,
        compiler_params=pltpu.CompilerParams(dimension_semantics=("parallel", "parallel"),
                                             vmem_limit_bytes=VMEM_LIMIT),
        name="pre",
    )(x, mod, *weights)


def _wkv_streams(streams):
    L = CHUNK
    t = lax.broadcasted_iota(jnp.int32, (L, GROUP), 0)
    lane = lax.broadcasted_iota(jnp.int32, (L, GROUP), 1)
    s = lane & (L - 1)
    eye = (s == t).astype(F32)
    head_masks = [((lane >> 6) == hh).astype(F32).astype(BF16) for hh in range(HEADS_PER_GROUP)]
    t2 = lax.broadcasted_iota(jnp.int32, (L, L), 0)
    s2 = lax.broadcasted_iota(jnp.int32, (L, L), 1)
    tri_f = (s2 <= t2).astype(F32).astype(BF16)
    tri_b = (s2 >= t2).astype(F32).astype(BF16)

    def sp(x, full=True):
        hi = x.astype(BF16)
        return hi, ((x - hi.astype(F32)).astype(BF16) if full else None)

    def bd1(q):
        return jnp.concatenate([q * m for m in head_masks], axis=0)

    def bd(q, full=True):
        qh, ql = sp(q, full)
        return bd1(qh), (None if ql is None else bd1(ql))

    def bdmm(parts, w, nt=False):
        wh, wl = w
        three = [p[1] is not None for p in parts]
        rows = [p[0] for p in parts] + [p[1] for p, f in zip(parts, three) if f]
        main = _dg(jnp.concatenate(rows, axis=0) if len(rows) > 1 else rows[0], wh, nt)
        cross_rows = [p[0] for p, f in zip(parts, three) if f] if wl is not None else []
        cross = _dg(jnp.concatenate(cross_rows, axis=0) if len(cross_rows) > 1 else cross_rows[0], wl, nt) if cross_rows else None
        outs, off = [], 0
        for p in parts:
            m = p[0].shape[0]
            outs.append(main[off:off + m])
            off += m
        lo_off, cr_off = off, 0
        for i, (p, f) in enumerate(zip(parts, three)):
            if not f:
                continue
            m = p[0].shape[0]
            outs[i] = outs[i] + main[lo_off:lo_off + m]
            lo_off += m
            if cross is not None:
                outs[i] = outs[i] + cross[cr_off:cr_off + m]
                cr_off += m
        return outs

    def each(fn, *cols):
        return [fn(*args) for args in zip(*cols)]

    r, kk, v, lw, kd, b, h0, bwd = (list(col) for col in zip(*streams))
    strict = [(s > t) if bw else (s < t) for bw in bwd]
    incl = [(s >= t) if bw else (s <= t) for bw in bwd]

    def cumsum(lw_, bw):
        tri = tri_b if bw else tri_f
        l1, l2 = sp(lw_)
        return _dg(tri, l1) + _dg(tri, l2)

    c = each(cumsum, lw, bwd)
    c_last = each(lambda c_, bw: c_[0:1, :] if bw else c_[L - 1:L, :], c, bwd)
    e_neg = each(lambda c_: jnp.exp(-c_), c)
    kt = each(lambda kk_, c_, lw_: kk_ * jnp.exp(c_ - lw_), kk, c, lw)
    rt = each(lambda r_, c_: r_ * jnp.exp(c_), r, c)
    kdt = each(lambda kd_, e_: kd_ * e_, kd, e_neg)
    bt = each(lambda b_, e_: b_ * e_, b, e_neg)
    eye_g = each(lambda cl: eye * jnp.exp(cl), c_last)
    lhs1 = each(lambda a1, a2, a3: [sp(a1, False), sp(a2, False), sp(a3, False)], kt, rt, eye_g)
    gk = each(lambda p, q: bdmm(p, bd(q, False), nt=True), lhs1, kdt)
    gb = each(lambda p, q: bdmm(p, bd(q, False), nt=True), lhs1, bt)
    a_k = each(lambda m, g_: jnp.where(m, g_[0], 0.0), strict, gk)
    a_rk = each(lambda m, g_: jnp.where(m, g_[1], 0.0), incl, gk)
    a_b = each(lambda m, g_: jnp.where(m, g_[0], 0.0), strict, gb)
    a_rb = each(lambda m, g_: jnp.where(m, g_[1], 0.0), incl, gb)

    mm = lambda p, q: bdmm([sp(p, False)], bd(q, False))[0]
    z = a_b
    m = 1
    while m < L:
        shift = m.bit_length() - 1
        join = ((t >> (shift + 1)) == (s >> (shift + 1))) & ((t >> shift) != (s >> shift))
        z = each(lambda z_: z_ - mm(jnp.where(join, z_, 0.0), z_), z)
        m *= 2

    gh = each(lambda p, eg, h_: bdmm(p[:2] + [sp(eg)], bd(h_)), lhs1, eye_g, h0)
    gv = each(lambda ak, ark, g_, v_: bdmm([sp(ak, False), sp(ark, False), sp(g_[2], False)], bd(v_, False)),
              a_k, a_rk, gk, v)
    rhs = each(lambda gh_, gv_: gh_[0] + gv_[0], gh, gv)
    u = each(lambda z_, rhs_: rhs_ - mm(z_, rhs_), z, rhs)
    gu = each(lambda arb, g_, u_: bdmm([sp(arb, False), sp(g_[2], False)], bd(u_, False)), a_rb, gb, u)
    y = each(lambda gh_, gv_, gu_: gh_[1] + gv_[1] - gu_[0], gh, gv, gu)
    h_next = each(lambda gh_, gv_, gu_: gh_[2] + gv_[2] - gu_[1], gh, gv, gu)
    return list(zip(y, h_next))


def _wkv_kernel(rf, kkf, vf, lwf, kdf, bf, rb, kkb, vb, lwb, kdb, bb, s0_ref, yf_ref, yb_ref, st_ref, h_ref):
    c = pl.program_id(1)

    @pl.when(c == 0)
    def _():
        h_ref[...] = s0_ref[...]

    dirs = ((rf, kkf, vf, lwf, kdf, bf, yf_ref), (rb, kkb, vb, lwb, kdb, bb, yb_ref))
    groups = [slice(g * GROUP, (g + 1) * GROUP) for g in range(N_GROUPS)]
    keys = [(i, d, sl) for i in range(WKV_BATCH) for d in range(2) for sl in groups]
    streams = [tuple(ref[i, :, sl] for ref in dirs[d][:6]) + (h_ref[i, d, :, sl], d == 1) for i, d, sl in keys]
    for (i, d, sl), (y, hn) in zip(keys, _wkv_streams(streams)):
        dirs[d][6][i, :, sl] = y
        h_ref[i, d, :, sl] = hn

    @pl.when(c == pl.num_programs(1) - 1)
    def _():
        st_ref[...] = h_ref[...]


def _wkv(r, kk, v, lw0, lw1, kd0, kd1, b0, b1, s0):
    B, T, _ = r.shape
    nc = T // CHUNK
    nb = WKV_BATCH
    assert B % nb == 0
    fwd = pl.BlockSpec((nb, CHUNK, A_WIDTH), lambda b, c: (b, c, 0))
    bwd = pl.BlockSpec((nb, CHUNK, A_WIDTH), lambda b, c: (b, nc - 1 - c, 0))
    st = pl.BlockSpec((nb, 2, HEAD_DIM, A_WIDTH), lambda b, c: (b, 0, 0, 0))
    return pl.pallas_call(
        _wkv_kernel,
        grid=(B // nb, nc),
        in_specs=[fwd] * 6 + [bwd] * 6 + [st],
        out_specs=[fwd, bwd, st],
        out_shape=[jax.ShapeDtypeStruct((B, T, A_WIDTH), F32), jax.ShapeDtypeStruct((B, T, A_WIDTH), F32),
                   jax.ShapeDtypeStruct((B, 2, HEAD_DIM, A_WIDTH), F32)],
        scratch_shapes=[pltpu.VMEM((nb, 2, HEAD_DIM, A_WIDTH), F32)],
        compiler_params=pltpu.CompilerParams(dimension_semantics=("parallel", "arbitrary"),
                                             vmem_limit_bytes=VMEM_LIMIT),
        name="wkv",
    )(r, kk, v, lw0, kd0, b0, r, kk, v, lw1, kd1, b1, s0)


def _post_kernel(*refs, latent, seq_len):
    if latent:
        (x_ref, mod_ref, yf_ref, yb_ref, bv_ref, ga_ref, cu_ref, cup_ref, cun_ref, bg_ref,
         gnw_ref, gnb_ref, cw_ref, wout_ref, fg_ref, ones_ref, o_ref) = refs
    else:
        (x_ref, mod_ref, yf_ref, yb_ref, bv_ref, ga_ref, cu_ref, bg_ref,
         gnw_ref, gnb_ref, cw_ref, wout_ref, fg_ref, ones_ref, o_ref) = refs
    x = x_ref[0]
    gate = mod_ref[0][:, 2 * D_MODEL:]
    y = yf_ref[0] + yb_ref[0]
    inv_n = 1.0 / HEAD_DIM
    mean = _segsum(y, ones_ref) * inv_n
    dlt = y - mean
    var = _segsum(dlt * dlt, ones_ref) * inv_n
    yn = dlt * lax.rsqrt(var + GN_EPS) * gnw_ref[...] + gnb_ref[...] + bv_ref[0].astype(F32)
    ya = yn * ga_ref[0].astype(F32)

    cu = cu_ref[0].astype(F32)
    w_prev, w_mid, w_next = cw_ref[0:1, :], cw_ref[1:2, :], cw_ref[2:3, :]
    if latent:
        half = B_WIDTH // 2
        t = pl.program_id(1)
        ph, nh = _shift_rows(cu[:, :half], GRID_W)
        top = jnp.where(t == 0, 0.0, cup_ref[0][:, half:].astype(F32))
        bot = jnp.where(t == pl.num_programs(1) - 1, 0.0, cun_ref[0][:, half:].astype(F32))
        pv = jnp.concatenate([top, cu[:cu.shape[0] - HALO, half:]], axis=0)
        nv = jnp.concatenate([cu[HALO:, half:], bot], axis=0)
        prev = jnp.concatenate([ph, pv], axis=1)
        nxt = jnp.concatenate([nh, nv], axis=1)
    else:
        prev, nxt = _shift_rows(cu, seq_len)
    yb = bg_ref[0].astype(F32) * (w_prev * prev + w_mid * cu + w_next * nxt)

    u = _dg(ya.astype(BF16), wout_ref[:A_WIDTH, :]) + _dg(yb.astype(BF16), wout_ref[A_WIDTH:, :])
    o = x + gate * u
    o_ref[0] = o * lax.rsqrt(jnp.mean(o * o, axis=-1, keepdims=True) + NORM_EPS) * fg_ref[...]


def _post(x, mod, yf, yb, bv, ga, cu, bg, weights, latent, seq_len):
    B, T, _ = x.shape
    tile = min(TOKEN_TILE, T)
    nt = T // tile
    tok = lambda w: pl.BlockSpec((1, tile, w), lambda b, t: (b, t, 0))
    mod_spec = pl.BlockSpec((1, 1, 3 * D_MODEL), (lambda b, t: (b, 0, 0)) if mod.shape[0] > 1 else (lambda b, t: (0, 0, 0)))
    per_tile = tile // HALO
    n_halo = T // HALO
    halo_prev = pl.BlockSpec((1, HALO, B_WIDTH), lambda b, t: (b, jnp.maximum(t * per_tile - 1, 0), 0))
    halo_next = pl.BlockSpec((1, HALO, B_WIDTH), lambda b, t: (b, jnp.minimum((t + 1) * per_tile, n_halo - 1), 0))
    acts = [x, mod, yf, yb, bv, ga, cu] + ([cu, cu] if latent else []) + [bg]
    specs = [tok(D_MODEL), mod_spec] + [tok(A_WIDTH)] * 5 + ([halo_prev, halo_next] if latent else []) + [tok(B_WIDTH)]
    return pl.pallas_call(
        functools.partial(_post_kernel, latent=latent, seq_len=seq_len),
        grid=(B, nt),
        in_specs=specs + [_const_spec(w.shape) for w in weights],
        out_specs=tok(D_MODEL),
        out_shape=jax.ShapeDtypeStruct((B, T, D_MODEL), F32),
        compiler_params=pltpu.CompilerParams(dimension_semantics=("parallel", "parallel"),
                                             vmem_limit_bytes=VMEM_LIMIT),
        name="post",
    )(*acts, *weights)


def _block_diag2(w):
    z = jnp.zeros_like(w[0])
    return jnp.concatenate([jnp.concatenate([w[0], z], axis=1), jnp.concatenate([z, w[1]], axis=1)], axis=0)


def kernel(x_prompt, x_sample, c, state_wkv, c_ctx, w_ada, b_ada, norm_g, w_in, shift_mu, decay_w0, decay_down,
           decay_up, iclr_bias, iclr_down, iclr_up, kk_scale, ka_scale, bonus_rk, gn_w, gn_b, conv_w, w_out, final_g):
    assert w_in.shape[0] == 1, "single layer"
    n_lat = c.shape[0]
    row = lambda z: z.reshape(1, -1)

    pad = (-(n_lat + 1)) % 8
    cvec = jnp.concatenate([c, c_ctx[None, :], jnp.zeros((pad, D_MODEL), F32)], axis=0)
    mod = _adaln(cvec, w_ada[0].astype(BF16), row(b_ada[0]))
    mod_lat = mod[:n_lat, None, :]
    mod_ctx = mod[n_lat:n_lat + 1, None, :]

    seg_id = np.arange(GROUP) // HEAD_DIM
    ones_bd = jnp.asarray((seg_id[:, None] == seg_id[None, :]).astype(np.float32), dtype=BF16)
    w_down = jnp.concatenate([decay_down[0, 0], decay_down[0, 1], iclr_down[0, 0], iclr_down[0, 1]], axis=1)
    pre_w = [row(norm_g[0]), w_in[0].astype(BF16), w_down.astype(BF16), _block_diag2(decay_up[0]).astype(BF16),
             _block_diag2(iclr_up[0]).astype(BF16), row(decay_w0[0]), row(iclr_bias[0]), shift_mu[0],
             row(kk_scale[0]), row(ka_scale[0]), row(bonus_rk[0]), ones_bd]
    post_w = [row(gn_w[0]), row(gn_b[0]), conv_w[0], w_out[0].astype(BF16), row(final_g), ones_bd]

    def to_state(s):
        return jnp.transpose(s, (0, 1, 4, 2, 3)).reshape(s.shape[0], 2, HEAD_DIM, A_WIDTH)

    def from_state(s):
        return jnp.transpose(s.reshape(s.shape[0], 2, HEAD_DIM, A_HEADS, HEAD_DIM), (0, 1, 3, 4, 2))

    def layer(x, mod_x, s0, latent):
        nb, t, _ = x.shape
        seg = GRID_W if latent else t
        fold = 1 if latent else math.gcd(nb, max(1, TOKEN_TILE // t))
        fold_in = lambda z: z.reshape(nb // fold, fold * t, z.shape[-1])
        outs = _pre(fold_in(x), mod_x, pre_w, seg)
        r, v, kk, lw0, lw1, kd0, kd1, b0, b1 = (z.reshape(nb, t, A_WIDTH) for z in outs[:9])
        bv, ga, cu, bg = outs[9:]
        yf, yb, st = _wkv(r, kk, v, lw0, lw1, kd0, kd1, b0, b1, s0)
        out = _post(fold_in(x), mod_x, fold_in(yf), fold_in(yb), bv, ga, cu, bg, post_w, latent, seg)
        return out.reshape(nb, t, D_MODEL), st

    assert x_prompt.shape[1] <= TOKEN_TILE, "a context sequence must fit one token tile (its conv has no halo)"
    s_zero = jnp.zeros((x_prompt.shape[0], 2, HEAD_DIM, A_WIDTH), F32)
    y_prompt, s_ctx = layer(x_prompt, mod_ctx, s_zero, False)
    y_sample, _ = layer(x_sample, mod_lat, to_state(state_wkv[:, 0]), True)
    new_state = from_state(s_ctx)[:, None]
    return (y_prompt, y_sample, new_state)
```

```python
import functools
import math

import numpy as np
import jax
import jax.numpy as jnp
from jax import lax
from jax.experimental import pallas as pl
from jax.experimental.pallas import tpu as pltpu

D_MODEL = 1024
A_WIDTH = 512
B_WIDTH = 512
HEAD_DIM = 64
A_HEADS = 8
GRID_W = 64
LORA = 64
IN_WIDTH = 4 * A_WIDTH + 4 * B_WIDTH
NORM_EPS = 1e-6
GN_EPS = 64e-5

CHUNK = 64
GROUP = 256
HEADS_PER_GROUP = GROUP // HEAD_DIM
N_GROUPS = A_WIDTH // GROUP
WKV_BATCH = 4
TOKEN_TILE = 512
HALO = GRID_W
VMEM_LIMIT = 56 * 1024 * 1024

F32 = jnp.float32
BF16 = jnp.bfloat16


def _dg(a, b, nt=False):
    dims = (((1,), (1 if nt else 0,)), ((), ()))
    return lax.dot_general(a, b, dims, preferred_element_type=F32)


def _sigmoid(x):
    return 0.5 * jnp.tanh(0.5 * x) + 0.5


def _silu(x):
    h = 0.5 * x
    return h * jnp.tanh(h) + h


def _segsum(z, ones_ref):
    zb = z.astype(BF16)
    return jnp.concatenate([_dg(zb[:, g * GROUP:(g + 1) * GROUP], ones_ref[...]) for g in range(N_GROUPS)], axis=1)


def _adaln_kernel(c_ref, w_ref, b_ref, o_ref):
    c = c_ref[...]
    s = _silu(c)
    o_ref[...] = _dg(s.astype(BF16), w_ref[...]) + b_ref[...]


def _adaln(cvec, w_ada_bf, b_ada):
    rows = cvec.shape[0]
    return pl.pallas_call(
        _adaln_kernel,
        out_shape=jax.ShapeDtypeStruct((rows, 3 * D_MODEL), F32),
        compiler_params=pltpu.CompilerParams(vmem_limit_bytes=VMEM_LIMIT),
        name="adaln",
    )(cvec, w_ada_bf, b_ada)


def _shift_rows(p, seg):
    rows = p.shape[0]
    assert seg & (seg - 1) == 0
    pos = lax.broadcasted_iota(jnp.int32, p.shape, 0) & (seg - 1)
    prev = jnp.where(pos == 0, 0.0, pltpu.roll(p, 1, 0))
    nxt = jnp.where(pos == seg - 1, 0.0, pltpu.roll(p, rows - 1, 0))
    return prev, nxt


def _pre_kernel(x_ref, mod_ref, g_ref, win_ref, wdn_ref, wdu_ref, wiu_ref, w0_ref, ab_ref, mu_ref,
                kks_ref, kas_ref, brk_ref, ones_ref,
                r_o, v_o, kk_o, lw0_o, lw1_o, kd0_o, kd1_o, b0_o, b1_o, bv_o, ga_o, cu_o, bg_o, *, seg):
    x = x_ref[0]
    mod = mod_ref[0]
    shift = mod[:, :D_MODEL]
    scale = mod[:, D_MODEL:2 * D_MODEL]
    gain = g_ref[...] * (1.0 + scale)
    h = x * lax.rsqrt(jnp.mean(x * x, axis=-1, keepdims=True) + NORM_EPS) * gain + shift
    hb = h.astype(BF16)

    def proj(i):
        return _dg(hb, win_ref[:, i * A_WIDTH:(i + 1) * A_WIDTH])

    def mixed(i):
        p = proj(i)
        prev, nxt = _shift_rows(p, seg)
        mu = mu_ref[i:i + 1, :]
        return p * (1.0 - mu) + (prev + nxt) * (0.5 * mu)

    low = _dg(hb, wdn_ref[...])
    lwz = jnp.tanh(low[:, :2 * LORA]).astype(BF16)
    la = low[:, 2 * LORA:].astype(BF16)
    wpre = _dg(lwz, wdu_ref[...]) + w0_ref[...]
    apre = _dg(la, wiu_ref[...]) + ab_ref[...]
    half_rate = -0.5 * math.exp(-0.5)
    lw = half_rate * jnp.tanh(0.5 * wpre) + half_rate
    a = _sigmoid(apre)
    lw0_o[0] = lw[:, :A_WIDTH]
    lw1_o[0] = lw[:, A_WIDTH:]

    r = mixed(0)
    k = mixed(1)
    v = mixed(2)
    r_o[0] = r
    v_o[0] = v

    ga = proj(3)
    ga_o[0] = _silu(ga).astype(BF16)
    bgate = proj(4)
    cgate = proj(5)
    uconv = proj(6)
    gb = proj(7)
    cu_o[0] = (cgate * uconv).astype(BF16)
    bg_o[0] = (bgate * _silu(gb)).astype(BF16)

    kks = k * kks_ref[...]
    ss = _segsum(kks * kks, ones_ref)
    kk = kks * lax.rsqrt(jnp.maximum(ss, 1e-24))
    kk_o[0] = kk
    kas = kas_ref[...]
    a0 = a[:, :A_WIDTH]
    a1 = a[:, A_WIDTH:]
    kd0 = k * (1.0 + (a0 - 1.0) * kas)
    kd1 = k * (1.0 + (a1 - 1.0) * kas)
    kd0_o[0] = kd0
    kd1_o[0] = kd1
    b0_o[0] = kk * a0
    b1_o[0] = kk * a1
    bonus = _segsum(r * (kd0 + kd1) * brk_ref[...], ones_ref)
    bv_o[0] = (bonus * v).astype(BF16)


def _const_spec(shape):
    nd = len(shape)
    return pl.BlockSpec(shape, lambda *_: (0,) * nd, pipeline_mode=pl.Buffered(1))


def _pre(x, mod, weights, seg):
    B, T, _ = x.shape
    tile = min(TOKEN_TILE, T)
    nt = T // tile
    tok = lambda w: pl.BlockSpec((1, tile, w), lambda b, t: (b, t, 0))
    mod_spec = pl.BlockSpec((1, 1, 3 * D_MODEL), (lambda b, t: (b, 0, 0)) if mod.shape[0] > 1 else (lambda b, t: (0, 0, 0)))
    n_f32, n_bf16 = 9, 4
    return pl.pallas_call(
        functools.partial(_pre_kernel, seg=seg),
        grid=(B, nt),
        in_specs=[tok(D_MODEL), mod_spec] + [_const_spec(w.shape) for w in weights],
        out_specs=[tok(A_WIDTH)] * (n_f32 + n_bf16),
        out_shape=[jax.ShapeDtypeStruct((B, T, A_WIDTH), F32)] * n_f32
        + [jax.ShapeDtypeStruct((B, T, A_WIDTH), BF16)] * n_bf16,
        compiler_params=pltpu.CompilerParams(dimension_semantics=("parallel", "parallel"),
                                             vmem_limit_bytes=VMEM_LIMIT),
        name="pre",
    )(x, mod, *weights)


def _wkv_streams(streams):
    L = CHUNK
    t = lax.broadcasted_iota(jnp.int32, (L, GROUP), 0)
    lane = lax.broadcasted_iota(jnp.int32, (L, GROUP), 1)
    s = lane & (L - 1)
    eye = (s == t).astype(F32)
    head_masks = [((lane >> 6) == hh).astype(F32).astype(BF16) for hh in range(HEADS_PER_GROUP)]
    t2 = lax.broadcasted_iota(jnp.int32, (L, L), 0)
    s2 = lax.broadcasted_iota(jnp.int32, (L, L), 1)
    tri_f = (s2 <= t2).astype(F32).astype(BF16)
    tri_b = (s2 >= t2).astype(F32).astype(BF16)

    def sp(x, full=True):
        hi = x.astype(BF16)
        return hi, ((x - hi.astype(F32)).astype(BF16) if full else None)

    def bd1(q):
        return jnp.concatenate([q * m for m in head_masks], axis=0)

    def bd(q, full=True):
        qh, ql = sp(q, full)
        return bd1(qh), (None if ql is None else bd1(ql))

    def bdmm(parts, w, nt=False):
        wh, wl = w
        three = [p[1] is not None for p in parts]
        rows = [p[0] for p in parts] + [p[1] for p, f in zip(parts, three) if f]
        main = _dg(jnp.concatenate(rows, axis=0) if len(rows) > 1 else rows[0], wh, nt)
        cross_rows = [p[0] for p, f in zip(parts, three) if f] if wl is not None else []
        cross = _dg(jnp.concatenate(cross_rows, axis=0) if len(cross_rows) > 1 else cross_rows[0], wl, nt) if cross_rows else None
        outs, off = [], 0
        for p in parts:
            m = p[0].shape[0]
            outs.append(main[off:off + m])
            off += m
        lo_off, cr_off = off, 0
        for i, (p, f) in enumerate(zip(parts, three)):
            if not f:
                continue
            m = p[0].shape[0]
            outs[i] = outs[i] + main[lo_off:lo_off + m]
            lo_off += m
            if cross is not None:
                outs[i] = outs[i] + cross[cr_off:cr_off + m]
                cr_off += m
        return outs

    def each(fn, *cols):
        return [fn(*args) for args in zip(*cols)]

    r, kk, v, lw, kd, b, h0, bwd = (list(col) for col in zip(*streams))
    strict = [(s > t) if bw else (s < t) for bw in bwd]
    incl = [(s >= t) if bw else (s <= t) for bw in bwd]

    def cumsum(lw_, bw):
        tri = tri_b if bw else tri_f
        l1, l2 = sp(lw_)
        return _dg(tri, l1) + _dg(tri, l2)

    c = each(cumsum, lw, bwd)
    c_last = each(lambda c_, bw: c_[0:1, :] if bw else c_[L - 1:L, :], c, bwd)
    e_neg = each(lambda c_: jnp.exp(-c_), c)
    kt = each(lambda kk_, c_, lw_: kk_ * jnp.exp(c_ - lw_), kk, c, lw)
    rt = each(lambda r_, c_: r_ * jnp.exp(c_), r, c)
    kdt = each(lambda kd_, e_: kd_ * e_, kd, e_neg)
    bt = each(lambda b_, e_: b_ * e_, b, e_neg)
    eye_g = each(lambda cl: eye * jnp.exp(cl), c_last)
    lhs1 = each(lambda a1, a2, a3: [sp(a1, False), sp(a2, False), sp(a3, False)], kt, rt, eye_g)
    gk = each(lambda p, q: bdmm(p, bd(q, False), nt=True), lhs1, kdt)
    gb = each(lambda p, q: bdmm(p, bd(q, False), nt=True), lhs1, bt)
    a_k = each(lambda m, g_: jnp.where(m, g_[0], 0.0), strict, gk)
    a_rk = each(lambda m, g_: jnp.where(m, g_[1], 0.0), incl, gk)
    a_b = each(lambda m, g_: jnp.where(m, g_[0], 0.0), strict, gb)
    a_rb = each(lambda m, g_: jnp.where(m, g_[1], 0.0), incl, gb)

    mm = lambda p, q: bdmm([sp(p, False)], bd(q, False))[0]
    z = a_b
    m = 1
    while m < L:
        shift = m.bit_length() - 1
        join = ((t >> (shift + 1)) == (s >> (shift + 1))) & ((t >> shift) != (s >> shift))
        z = each(lambda z_: z_ - mm(jnp.where(join, z_, 0.0), z_), z)
        m *= 2

    gh = each(lambda p, eg, h_: bdmm(p[:2] + [sp(eg)], bd(h_)), lhs1, eye_g, h0)
    gv = each(lambda ak, ark, g_, v_: bdmm([sp(ak, False), sp(ark, False), sp(g_[2], False)], bd(v_, False)),
              a_k, a_rk, gk, v)
    rhs = each(lambda gh_, gv_: gh_[0] + gv_[0], gh, gv)
    u = each(lambda z_, rhs_: rhs_ - mm(z_, rhs_), z, rhs)
    gu = each(lambda arb, g_, u_: bdmm([sp(arb, False), sp(g_[2], False)], bd(u_, False)), a_rb, gb, u)
    y = each(lambda gh_, gv_, gu_: gh_[1] + gv_[1] - gu_[0], gh, gv, gu)
    h_next = each(lambda gh_, gv_, gu_: gh_[2] + gv_[2] - gu_[1], gh, gv, gu)
    return list(zip(y, h_next))


def _wkv_kernel(rf, kkf, vf, lwf, kdf, bf, rb, kkb, vb, lwb, kdb, bb, s0_ref, yf_ref, yb_ref, st_ref, h_ref):
    c = pl.program_id(1)

    @pl.when(c == 0)
    def _():
        h_ref[...] = s0_ref[...]

    dirs = ((rf, kkf, vf, lwf, kdf, bf, yf_ref), (rb, kkb, vb, lwb, kdb, bb, yb_ref))
    groups = [slice(g * GROUP, (g + 1) * GROUP) for g in range(N_GROUPS)]
    keys = [(i, d, sl) for i in range(WKV_BATCH) for d in range(2) for sl in groups]
    streams = [tuple(ref[i, :, sl] for ref in dirs[d][:6]) + (h_ref[i, d, :, sl], d == 1) for i, d, sl in keys]
    for (i, d, sl), (y, hn) in zip(keys, _wkv_streams(streams)):
        dirs[d][6][i, :, sl] = y
        h_ref[i, d, :, sl] = hn

    @pl.when(c == pl.num_programs(1) - 1)
    def _():
        st_ref[...] = h_ref[...]


def _wkv(r, kk, v, lw0, lw1, kd0, kd1, b0, b1, s0):
    B, T, _ = r.shape
    nc = T // CHUNK
    nb = WKV_BATCH
    assert B % nb == 0
    fwd = pl.BlockSpec((nb, CHUNK, A_WIDTH), lambda b, c: (b, c, 0))
    bwd = pl.BlockSpec((nb, CHUNK, A_WIDTH), lambda b, c: (b, nc - 1 - c, 0))
    st = pl.BlockSpec((nb, 2, HEAD_DIM, A_WIDTH), lambda b, c: (b, 0, 0, 0))
    return pl.pallas_call(
        _wkv_kernel,
        grid=(B // nb, nc),
        in_specs=[fwd] * 6 + [bwd] * 6 + [st],
        out_specs=[fwd, bwd, st],
        out_shape=[jax.ShapeDtypeStruct((B, T, A_WIDTH), F32), jax.ShapeDtypeStruct((B, T, A_WIDTH), F32),
                   jax.ShapeDtypeStruct((B, 2, HEAD_DIM, A_WIDTH), F32)],
        scratch_shapes=[pltpu.VMEM((nb, 2, HEAD_DIM, A_WIDTH), F32)],
        compiler_params=pltpu.CompilerParams(dimension_semantics=("parallel", "arbitrary"),
                                             vmem_limit_bytes=VMEM_LIMIT),
        name="wkv",
    )(r, kk, v, lw0, kd0, b0, r, kk, v, lw1, kd1, b1, s0)


def _post_kernel(*refs, latent, seq_len):
    if latent:
        (x_ref, mod_ref, yf_ref, yb_ref, bv_ref, ga_ref, cu_ref, cup_ref, cun_ref, bg_ref,
         gnw_ref, gnb_ref, cw_ref, wout_ref, fg_ref, ones_ref, o_ref) = refs
    else:
        (x_ref, mod_ref, yf_ref, yb_ref, bv_ref, ga_ref, cu_ref, bg_ref,
         gnw_ref, gnb_ref, cw_ref, wout_ref, fg_ref, ones_ref, o_ref) = refs
    x = x_ref[0]
    gate = mod_ref[0][:, 2 * D_MODEL:]
    y = yf_ref[0] + yb_ref[0]
    inv_n = 1.0 / HEAD_DIM
    mean = _segsum(y, ones_ref) * inv_n
    dlt = y - mean
    var = _segsum(dlt * dlt, ones_ref) * inv_n
    yn = dlt * lax.rsqrt(var + GN_EPS) * gnw_ref[...] + gnb_ref[...] + bv_ref[0].astype(F32)
    ya = yn * ga_ref[0].astype(F32)

    cu = cu_ref[0].astype(F32)
    w_prev, w_mid, w_next = cw_ref[0:1, :], cw_ref[1:2, :], cw_ref[2:3, :]
    if latent:
        half = B_WIDTH // 2
        t = pl.program_id(1)
        ph, nh = _shift_rows(cu[:, :half], GRID_W)
        top = jnp.where(t == 0, 0.0, cup_ref[0][:, half:].astype(F32))
        bot = jnp.where(t == pl.num_programs(1) - 1, 0.0, cun_ref[0][:, half:].astype(F32))
        pv = jnp.concatenate([top, cu[:cu.shape[0] - HALO, half:]], axis=0)
        nv = jnp.concatenate([cu[HALO:, half:], bot], axis=0)
        prev = jnp.concatenate([ph, pv], axis=1)
        nxt = jnp.concatenate([nh, nv], axis=1)
    else:
        prev, nxt = _shift_rows(cu, seq_len)
    yb = bg_ref[0].astype(F32) * (w_prev * prev + w_mid * cu + w_next * nxt)

    u = _dg(ya.astype(BF16), wout_ref[:A_WIDTH, :]) + _dg(yb.astype(BF16), wout_ref[A_WIDTH:, :])
    o = x + gate * u
    o_ref[0] = o * lax.rsqrt(jnp.mean(o * o, axis=-1, keepdims=True) + NORM_EPS) * fg_ref[...]


def _post(x, mod, yf, yb, bv, ga, cu, bg, weights, latent, seq_len):
    B, T, _ = x.shape
    tile = min(TOKEN_TILE, T)
    nt = T // tile
    tok = lambda w: pl.BlockSpec((1, tile, w), lambda b, t: (b, t, 0))
    mod_spec = pl.BlockSpec((1, 1, 3 * D_MODEL), (lambda b, t: (b, 0, 0)) if mod.shape[0] > 1 else (lambda b, t: (0, 0, 0)))
    per_tile = tile // HALO
    n_halo = T // HALO
    halo_prev = pl.BlockSpec((1, HALO, B_WIDTH), lambda b, t: (b, jnp.maximum(t * per_tile - 1, 0), 0))
    halo_next = pl.BlockSpec((1, HALO, B_WIDTH), lambda b, t: (b, jnp.minimum((t + 1) * per_tile, n_halo - 1), 0))
    acts = [x, mod, yf, yb, bv, ga, cu] + ([cu, cu] if latent else []) + [bg]
    specs = [tok(D_MODEL), mod_spec] + [tok(A_WIDTH)] * 5 + ([halo_prev, halo_next] if latent else []) + [tok(B_WIDTH)]
    return pl.pallas_call(
        functools.partial(_post_kernel, latent=latent, seq_len=seq_len),
        grid=(B, nt),
        in_specs=specs + [_const_spec(w.shape) for w in weights],
        out_specs=tok(D_MODEL),
        out_shape=jax.ShapeDtypeStruct((B, T, D_MODEL), F32),
        compiler_params=pltpu.CompilerParams(dimension_semantics=("parallel", "parallel"),
                                             vmem_limit_bytes=VMEM_LIMIT),
        name="post",
    )(*acts, *weights)


def _block_diag2(w):
    z = jnp.zeros_like(w[0])
    return jnp.concatenate([jnp.concatenate([w[0], z], axis=1), jnp.concatenate([z, w[1]], axis=1)], axis=0)


def kernel(x_prompt, x_sample, c, state_wkv, c_ctx, w_ada, b_ada, norm_g, w_in, shift_mu, decay_w0, decay_down,
           decay_up, iclr_bias, iclr_down, iclr_up, kk_scale, ka_scale, bonus_rk, gn_w, gn_b, conv_w, w_out, final_g):
    assert w_in.shape[0] == 1, "single layer"
    n_lat = c.shape[0]
    row = lambda z: z.reshape(1, -1)

    pad = (-(n_lat + 1)) % 8
    cvec = jnp.concatenate([c, c_ctx[None, :], jnp.zeros((pad, D_MODEL), F32)], axis=0)
    mod = _adaln(cvec, w_ada[0].astype(BF16), row(b_ada[0]))
    mod_lat = mod[:n_lat, None, :]
    mod_ctx = mod[n_lat:n_lat + 1, None, :]

    seg_id = np.arange(GROUP) // HEAD_DIM
    ones_bd = jnp.asarray((seg_id[:, None] == seg_id[None, :]).astype(np.float32), dtype=BF16)
    w_down = jnp.concatenate([decay_down[0, 0], decay_down[0, 1], iclr_down[0, 0], iclr_down[0, 1]], axis=1)
    pre_w = [row(norm_g[0]), w_in[0].astype(BF16), w_down.astype(BF16), _block_diag2(decay_up[0]).astype(BF16),
             _block_diag2(iclr_up[0]).astype(BF16), row(decay_w0[0]), row(iclr_bias[0]), shift_mu[0],
             row(kk_scale[0]), row(ka_scale[0]), row(bonus_rk[0]), ones_bd]
    post_w = [row(gn_w[0]), row(gn_b[0]), conv_w[0], w_out[0].astype(BF16), row(final_g), ones_bd]

    def to_state(s):
        return jnp.transpose(s, (0, 1, 4, 2, 3)).reshape(s.shape[0], 2, HEAD_DIM, A_WIDTH)

    def from_state(s):
        return jnp.transpose(s.reshape(s.shape[0], 2, HEAD_DIM, A_HEADS, HEAD_DIM), (0, 1, 3, 4, 2))

    def layer(x, mod_x, s0, latent):
        nb, t, _ = x.shape
        seg = GRID_W if latent else t
        fold = 1 if latent else math.gcd(nb, max(1, TOKEN_TILE // t))
        fold_in = lambda z: z.reshape(nb // fold, fold * t, z.shape[-1])
        outs = _pre(fold_in(x), mod_x, pre_w, seg)
        r, v, kk, lw0, lw1, kd0, kd1, b0, b1 = (z.reshape(nb, t, A_WIDTH) for z in outs[:9])
        bv, ga, cu, bg = outs[9:]
        yf, yb, st = _wkv(r, kk, v, lw0, lw1, kd0, kd1, b0, b1, s0)
        out = _post(fold_in(x), mod_x, fold_in(yf), fold_in(yb), bv, ga, cu, bg, post_w, latent, seg)
        return out.reshape(nb, t, D_MODEL), st

    assert x_prompt.shape[1] <= TOKEN_TILE, "a context sequence must fit one token tile (its conv has no halo)"
    s_zero = jnp.zeros((x_prompt.shape[0], 2, HEAD_DIM, A_WIDTH), F32)
    y_prompt, s_ctx = layer(x_prompt, mod_ctx, s_zero, False)
    y_sample, _ = layer(x_sample, mod_lat, to_state(state_wkv[:, 0]), True)
    new_state = from_state(s_ctx)[:, None]
    return (y_prompt, y_sample, new_state)
```

```python
import functools
import math

import numpy as np
import jax
import jax.numpy as jnp
from jax import lax
from jax.experimental import pallas as pl
from jax.experimental.pallas import tpu as pltpu

D_MODEL = 1024
A_WIDTH = 512
B_WIDTH = 512
HEAD_DIM = 64
A_HEADS = 8
GRID_W = 64
LORA = 64
IN_WIDTH = 4 * A_WIDTH + 4 * B_WIDTH
NORM_EPS = 1e-6
GN_EPS = 64e-5

CHUNK = 64
GROUP = 256
HEADS_PER_GROUP = GROUP // HEAD_DIM
N_GROUPS = A_WIDTH // GROUP
WKV_BATCH = 4
TOKEN_TILE = 512
HALO = GRID_W
VMEM_LIMIT = 56 * 1024 * 1024

F32 = jnp.float32
BF16 = jnp.bfloat16


def _dg(a, b, nt=False):
    dims = (((1,), (1 if nt else 0,)), ((), ()))
    return lax.dot_general(a, b, dims, preferred_element_type=F32)


def _sigmoid(x):
    return 0.5 * jnp.tanh(0.5 * x) + 0.5


def _silu(x):
    h = 0.5 * x
    return h * jnp.tanh(h) + h


def _segsum(z, ones_ref):
    zb = z.astype(BF16)
    return jnp.concatenate([_dg(zb[:, g * GROUP:(g + 1) * GROUP], ones_ref[...]) for g in range(N_GROUPS)], axis=1)


def _adaln_kernel(c_ref, w_ref, b_ref, o_ref):
    c = c_ref[...]
    s = _silu(c)
    o_ref[...] = _dg(s.astype(BF16), w_ref[...]) + b_ref[...]


def _adaln(cvec, w_ada_bf, b_ada):
    rows = cvec.shape[0]
    return pl.pallas_call(
        _adaln_kernel,
        out_shape=jax.ShapeDtypeStruct((rows, 3 * D_MODEL), F32),
        compiler_params=pltpu.CompilerParams(vmem_limit_bytes=VMEM_LIMIT),
        name="adaln",
    )(cvec, w_ada_bf, b_ada)


def _shift_rows(p, seg):
    rows = p.shape[0]
    assert seg & (seg - 1) == 0
    pos = lax.broadcasted_iota(jnp.int32, p.shape, 0) & (seg - 1)
    prev = jnp.where(pos == 0, 0.0, pltpu.roll(p, 1, 0))
    nxt = jnp.where(pos == seg - 1, 0.0, pltpu.roll(p, rows - 1, 0))
    return prev, nxt


def _pre_kernel(x_ref, mod_ref, g_ref, win_ref, wdn_ref, wdu_ref, wiu_ref, w0_ref, ab_ref, mu_ref,
                kks_ref, kas_ref, brk_ref, ones_ref,
                r_o, v_o, kk_o, lw0_o, lw1_o, kd0_o, kd1_o, b0_o, b1_o, bv_o, ga_o, cu_o, bg_o, *, seg):
    x = x_ref[0]
    mod = mod_ref[0]
    shift = mod[:, :D_MODEL]
    scale = mod[:, D_MODEL:2 * D_MODEL]
    gain = g_ref[...] * (1.0 + scale)
    h = x * lax.rsqrt(jnp.mean(x * x, axis=-1, keepdims=True) + NORM_EPS) * gain + shift
    hb = h.astype(BF16)

    def proj(i):
        return _dg(hb, win_ref[:, i * A_WIDTH:(i + 1) * A_WIDTH])

    def mixed(i):
        p = proj(i)
        prev, nxt = _shift_rows(p, seg)
        mu = mu_ref[i:i + 1, :]
        return p * (1.0 - mu) + (prev + nxt) * (0.5 * mu)

    low = _dg(hb, wdn_ref[...])
    lwz = jnp.tanh(low[:, :2 * LORA]).astype(BF16)
    la = low[:, 2 * LORA:].astype(BF16)
    wpre = _dg(lwz, wdu_ref[...]) + w0_ref[...]
    apre = _dg(la, wiu_ref[...]) + ab_ref[...]
    half_rate = -0.5 * math.exp(-0.5)
    lw = half_rate * jnp.tanh(0.5 * wpre) + half_rate
    a = _sigmoid(apre)
    lw0_o[0] = lw[:, :A_WIDTH]
    lw1_o[0] = lw[:, A_WIDTH:]

    r = mixed(0)
    k = mixed(1)
    v = mixed(2)
    r_o[0] = r
    v_o[0] = v

    ga = proj(3)
    ga_o[0] = _silu(ga).astype(BF16)
    bgate = proj(4)
    cgate = proj(5)
    uconv = proj(6)
    gb = proj(7)
    cu_o[0] = (cgate * uconv).astype(BF16)
    bg_o[0] = (bgate * _silu(gb)).astype(BF16)

    kks = k * kks_ref[...]
    ss = _segsum(kks * kks, ones_ref)
    kk = kks * lax.rsqrt(jnp.maximum(ss, 1e-24))
    kk_o[0] = kk
    kas = kas_ref[...]
    a0 = a[:, :A_WIDTH]
    a1 = a[:, A_WIDTH:]
    kd0 = k * (1.0 + (a0 - 1.0) * kas)
    kd1 = k * (1.0 + (a1 - 1.0) * kas)
    kd0_o[0] = kd0
    kd1_o[0] = kd1
    b0_o[0] = kk * a0
    b1_o[0] = kk * a1
    bonus = _segsum(r * (kd0 + kd1) * brk_ref[...], ones_ref)
    bv_o[0] = (bonus * v).astype(BF16)


def _const_spec(shape):
    nd = len(shape)
    return pl.BlockSpec(shape, lambda *_: (0,) * nd, pipeline_mode=pl.Buffered(1))


def _pre(x, mod, weights, seg):
    B, T, _ = x.shape
    tile = min(TOKEN_TILE, T)
    nt = T // tile
    tok = lambda w: pl.BlockSpec((1, tile, w), lambda b, t: (b, t, 0))
    mod_spec = pl.BlockSpec((1, 1, 3 * D_MODEL), (lambda b, t: (b, 0, 0)) if mod.shape[0] > 1 else (lambda b, t: (0, 0, 0)))
    n_f32, n_bf16 = 9, 4
    return pl.pallas_call(
        functools.partial(_pre_kernel, seg=seg),
        grid=(B, nt),
        in_specs=[tok(D_MODEL), mod_spec] + [_const_spec(w.shape) for w in weights],
        out_specs=[tok(A_WIDTH)] * (n_f32 + n_bf16),
        out_shape=[jax.ShapeDtypeStruct((B, T, A_WIDTH), F32)] * n_f32
        + [jax.ShapeDtypeStruct((B, T, A_WIDTH), BF16)] * n_bf16,
        compiler_params=pltpu.CompilerParams(dimension_semantics=("parallel", "parallel"),
                                             vmem_limit_bytes=VMEM_LIMIT),
        name="pre",
    )(x, mod, *weights)


def _wkv_streams(streams):
    L = CHUNK
    t = lax.broadcasted_iota(jnp.int32, (L, GROUP), 0)
    lane = lax.broadcasted_iota(jnp.int32, (L, GROUP), 1)
    s = lane & (L - 1)
    eye = (s == t).astype(F32)
    half = GROUP // 2
    lane_h = lax.broadcasted_iota(jnp.int32, (L, half), 1)
    half_masks = [((lane_h >> 6) == hh).astype(F32).astype(BF16) for hh in range(2)]
    zero_blk = jnp.zeros((2 * L, half), BF16)
    t2 = lax.broadcasted_iota(jnp.int32, (L, L), 0)
    s2 = lax.broadcasted_iota(jnp.int32, (L, L), 1)
    tri_f = (s2 <= t2).astype(F32).astype(BF16)
    tri_b = (s2 >= t2).astype(F32).astype(BF16)

    def bf(x):
        return x.astype(BF16)

    def bd(q):
        q_left, q_right = bf(q[:, :half]), bf(q[:, half:])
        left = jnp.concatenate([q_left * half_masks[0], q_left * half_masks[1], zero_blk], axis=0)
        right = jnp.concatenate([zero_blk, q_right * half_masks[0], q_right * half_masks[1]], axis=0)
        return jnp.concatenate([left, right], axis=1)

    def bdmm(parts, q, nt=False):
        rows = bf(parts[0]) if len(parts) == 1 else jnp.concatenate([bf(p) for p in parts], axis=0)
        out = _dg(rows, bd(q), nt)
        return [out[i * L:(i + 1) * L] for i in range(len(parts))]

    def each(fn, *cols):
        return [fn(*args) for args in zip(*cols)]

    r, kk, v, lw, kd, b, h0, bwd = (list(col) for col in zip(*streams))
    strict = [(s > t) if bw else (s < t) for bw in bwd]
    incl = [(s >= t) if bw else (s <= t) for bw in bwd]

    def cumsum(lw_, bw):
        tri = tri_b if bw else tri_f
        hi = bf(lw_)
        return _dg(tri, hi) + _dg(tri, bf(lw_ - hi.astype(F32)))

    c = each(cumsum, lw, bwd)
    c_last = each(lambda c_, bw: c_[0:1, :] if bw else c_[L - 1:L, :], c, bwd)
    e_neg = each(lambda c_: jnp.exp(-c_), c)
    kt = each(lambda kk_, c_, lw_: kk_ * jnp.exp(c_ - lw_), kk, c, lw)
    rt = each(lambda r_, c_: r_ * jnp.exp(c_), r, c)
    kdt = each(lambda kd_, e_: kd_ * e_, kd, e_neg)
    bt = each(lambda b_, e_: b_ * e_, b, e_neg)
    eye_g = each(lambda cl: eye * jnp.exp(cl), c_last)
    lhs1 = each(lambda a1, a2, a3: [a1, a2, a3], kt, rt, eye_g)
    gk = each(lambda p, q: bdmm(p, q, nt=True), lhs1, kdt)
    gb = each(lambda p, q: bdmm(p, q, nt=True), lhs1, bt)
    a_k = each(lambda m, g_: jnp.where(m, g_[0], 0.0), strict, gk)
    a_rk = each(lambda m, g_: jnp.where(m, g_[1], 0.0), incl, gk)
    a_b = each(lambda m, g_: jnp.where(m, g_[0], 0.0), strict, gb)
    a_rb = each(lambda m, g_: jnp.where(m, g_[1], 0.0), incl, gb)

    mm = lambda p, q: bdmm([p], q)[0]
    z = a_b
    m = 1
    while m < L:
        shift = m.bit_length() - 1
        join = ((t >> (shift + 1)) == (s >> (shift + 1))) & ((t >> shift) != (s >> shift))
        z = each(lambda z_: z_ - mm(jnp.where(join, z_, 0.0), z_), z)
        m *= 2

    gh = each(bdmm, lhs1, h0)
    gv = each(lambda ak, ark, g_, v_: bdmm([ak, ark, g_[2]], v_), a_k, a_rk, gk, v)
    rhs = each(lambda gh_, gv_: gh_[0] + gv_[0], gh, gv)
    u = each(lambda z_, rhs_: rhs_ - mm(z_, rhs_), z, rhs)
    gu = each(lambda arb, g_, u_: bdmm([arb, g_[2]], u_), a_rb, gb, u)
    y = each(lambda gh_, gv_, gu_: gh_[1] + gv_[1] - gu_[0], gh, gv, gu)
    h_next = each(lambda gh_, gv_, gu_: gh_[2] + gv_[2] - gu_[1], gh, gv, gu)
    return list(zip(y, h_next))


def _wkv_kernel(rf, kkf, vf, lwf, kdf, bf, rb, kkb, vb, lwb, kdb, bb, s0_ref, yf_ref, yb_ref, st_ref, h_ref):
    c = pl.program_id(1)

    @pl.when(c == 0)
    def _():
        h_ref[...] = s0_ref[...]

    dirs = ((rf, kkf, vf, lwf, kdf, bf, yf_ref), (rb, kkb, vb, lwb, kdb, bb, yb_ref))
    groups = [slice(g * GROUP, (g + 1) * GROUP) for g in range(N_GROUPS)]
    keys = [(i, d, sl) for i in range(WKV_BATCH) for d in range(2) for sl in groups]
    streams = [tuple(ref[i, :, sl] for ref in dirs[d][:6]) + (h_ref[i, d, :, sl], d == 1) for i, d, sl in keys]
    for (i, d, sl), (y, hn) in zip(keys, _wkv_streams(streams)):
        dirs[d][6][i, :, sl] = y
        h_ref[i, d, :, sl] = hn

    @pl.when(c == pl.num_programs(1) - 1)
    def _():
        st_ref[...] = h_ref[...]


def _wkv(r, kk, v, lw0, lw1, kd0, kd1, b0, b1, s0):
    B, T, _ = r.shape
    nc = T // CHUNK
    nb = WKV_BATCH
    assert B % nb == 0
    fwd = pl.BlockSpec((nb, CHUNK, A_WIDTH), lambda b, c: (b, c, 0))
    bwd = pl.BlockSpec((nb, CHUNK, A_WIDTH), lambda b, c: (b, nc - 1 - c, 0))
    st = pl.BlockSpec((nb, 2, HEAD_DIM, A_WIDTH), lambda b, c: (b, 0, 0, 0))
    return pl.pallas_call(
        _wkv_kernel,
        grid=(B // nb, nc),
        in_specs=[fwd] * 6 + [bwd] * 6 + [st],
        out_specs=[fwd, bwd, st],
        out_shape=[jax.ShapeDtypeStruct((B, T, A_WIDTH), F32), jax.ShapeDtypeStruct((B, T, A_WIDTH), F32),
                   jax.ShapeDtypeStruct((B, 2, HEAD_DIM, A_WIDTH), F32)],
        scratch_shapes=[pltpu.VMEM((nb, 2, HEAD_DIM, A_WIDTH), F32)],
        compiler_params=pltpu.CompilerParams(dimension_semantics=("parallel", "arbitrary"),
                                             vmem_limit_bytes=VMEM_LIMIT),
        name="wkv",
    )(r, kk, v, lw0, kd0, b0, r, kk, v, lw1, kd1, b1, s0)


def _post_kernel(*refs, latent, seq_len):
    if latent:
        (x_ref, mod_ref, yf_ref, yb_ref, bv_ref, ga_ref, cu_ref, cup_ref, cun_ref, bg_ref,
         gnw_ref, gnb_ref, cw_ref, wout_ref, fg_ref, ones_ref, o_ref) = refs
    else:
        (x_ref, mod_ref, yf_ref, yb_ref, bv_ref, ga_ref, cu_ref, bg_ref,
         gnw_ref, gnb_ref, cw_ref, wout_ref, fg_ref, ones_ref, o_ref) = refs
    x = x_ref[0]
    gate = mod_ref[0][:, 2 * D_MODEL:]
    y = yf_ref[0] + yb_ref[0]
    inv_n = 1.0 / HEAD_DIM
    mean = _segsum(y, ones_ref) * inv_n
    dlt = y - mean
    var = _segsum(dlt * dlt, ones_ref) * inv_n
    yn = dlt * lax.rsqrt(var + GN_EPS) * gnw_ref[...] + gnb_ref[...] + bv_ref[0].astype(F32)
    ya = yn * ga_ref[0].astype(F32)

    cu = cu_ref[0].astype(F32)
    w_prev, w_mid, w_next = cw_ref[0:1, :], cw_ref[1:2, :], cw_ref[2:3, :]
    if latent:
        half = B_WIDTH // 2
        t = pl.program_id(1)
        ph, nh = _shift_rows(cu[:, :half], GRID_W)
        top = jnp.where(t == 0, 0.0, cup_ref[0][:, half:].astype(F32))
        bot = jnp.where(t == pl.num_programs(1) - 1, 0.0, cun_ref[0][:, half:].astype(F32))
        pv = jnp.concatenate([top, cu[:cu.shape[0] - HALO, half:]], axis=0)
        nv = jnp.concatenate([cu[HALO:, half:], bot], axis=0)
        prev = jnp.concatenate([ph, pv], axis=1)
        nxt = jnp.concatenate([nh, nv], axis=1)
    else:
        prev, nxt = _shift_rows(cu, seq_len)
    yb = bg_ref[0].astype(F32) * (w_prev * prev + w_mid * cu + w_next * nxt)

    u = _dg(ya.astype(BF16), wout_ref[:A_WIDTH, :]) + _dg(yb.astype(BF16), wout_ref[A_WIDTH:, :])
    o = x + gate * u
    o_ref[0] = o * lax.rsqrt(jnp.mean(o * o, axis=-1, keepdims=True) + NORM_EPS) * fg_ref[...]


def _post(x, mod, yf, yb, bv, ga, cu, bg, weights, latent, seq_len):
    B, T, _ = x.shape
    tile = min(TOKEN_TILE, T)
    nt = T // tile
    tok = lambda w: pl.BlockSpec((1, tile, w), lambda b, t: (b, t, 0))
    mod_spec = pl.BlockSpec((1, 1, 3 * D_MODEL), (lambda b, t: (b, 0, 0)) if mod.shape[0] > 1 else (lambda b, t: (0, 0, 0)))
    per_tile = tile // HALO
    n_halo = T // HALO
    halo_prev = pl.BlockSpec((1, HALO, B_WIDTH), lambda b, t: (b, jnp.maximum(t * per_tile - 1, 0), 0))
    halo_next = pl.BlockSpec((1, HALO, B_WIDTH), lambda b, t: (b, jnp.minimum((t + 1) * per_tile, n_halo - 1), 0))
    acts = [x, mod, yf, yb, bv, ga, cu] + ([cu, cu] if latent else []) + [bg]
    specs = [tok(D_MODEL), mod_spec] + [tok(A_WIDTH)] * 5 + ([halo_prev, halo_next] if latent else []) + [tok(B_WIDTH)]
    return pl.pallas_call(
        functools.partial(_post_kernel, latent=latent, seq_len=seq_len),
        grid=(B, nt),
        in_specs=specs + [_const_spec(w.shape) for w in weights],
        out_specs=tok(D_MODEL),
        out_shape=jax.ShapeDtypeStruct((B, T, D_MODEL), F32),
        compiler_params=pltpu.CompilerParams(dimension_semantics=("parallel", "parallel"),
                                             vmem_limit_bytes=VMEM_LIMIT),
        name="post",
    )(*acts, *weights)


def _block_diag2(w):
    z = jnp.zeros_like(w[0])
    return jnp.concatenate([jnp.concatenate([w[0], z], axis=1), jnp.concatenate([z, w[1]], axis=1)], axis=0)


def kernel(x_prompt, x_sample, c, state_wkv, c_ctx, w_ada, b_ada, norm_g, w_in, shift_mu, decay_w0, decay_down,
           decay_up, iclr_bias, iclr_down, iclr_up, kk_scale, ka_scale, bonus_rk, gn_w, gn_b, conv_w, w_out, final_g):
    assert w_in.shape[0] == 1, "single layer"
    n_lat = c.shape[0]
    row = lambda z: z.reshape(1, -1)

    pad = (-(n_lat + 1)) % 8
    cvec = jnp.concatenate([c, c_ctx[None, :], jnp.zeros((pad, D_MODEL), F32)], axis=0)
    mod = _adaln(cvec, w_ada[0].astype(BF16), row(b_ada[0]))
    mod_lat = mod[:n_lat, None, :]
    mod_ctx = mod[n_lat:n_lat + 1, None, :]

    seg_id = np.arange(GROUP) // HEAD_DIM
    ones_bd = jnp.asarray((seg_id[:, None] == seg_id[None, :]).astype(np.float32), dtype=BF16)
    w_down = jnp.concatenate([decay_down[0, 0], decay_down[0, 1], iclr_down[0, 0], iclr_down[0, 1]], axis=1)
    pre_w = [row(norm_g[0]), w_in[0].astype(BF16), w_down.astype(BF16), _block_diag2(decay_up[0]).astype(BF16),
             _block_diag2(iclr_up[0]).astype(BF16), row(decay_w0[0]), row(iclr_bias[0]), shift_mu[0],
             row(kk_scale[0]), row(ka_scale[0]), row(bonus_rk[0]), ones_bd]
    post_w = [row(gn_w[0]), row(gn_b[0]), conv_w[0], w_out[0].astype(BF16), row(final_g), ones_bd]

    def to_state(s):
        return jnp.transpose(s, (0, 1, 4, 2, 3)).reshape(s.shape[0], 2, HEAD_DIM, A_WIDTH)

    def from_state(s):
        return jnp.transpose(s.reshape(s.shape[0], 2, HEAD_DIM, A_HEADS, HEAD_DIM), (0, 1, 3, 4, 2))

    def layer(x, mod_x, s0, latent):
        nb, t, _ = x.shape
        seg = GRID_W if latent else t
        fold = 1 if latent else math.gcd(nb, max(1, TOKEN_TILE // t))
        fold_in = lambda z: z.reshape(nb // fold, fold * t, z.shape[-1])
        outs = _pre(fold_in(x), mod_x, pre_w, seg)
        r, v, kk, lw0, lw1, kd0, kd1, b0, b1 = (z.reshape(nb, t, A_WIDTH) for z in outs[:9])
        bv, ga, cu, bg = outs[9:]
        yf, yb, st = _wkv(r, kk, v, lw0, lw1, kd0, kd1, b0, b1, s0)
        out = _post(fold_in(x), mod_x, fold_in(yf), fold_in(yb), bv, ga, cu, bg, post_w, latent, seg)
        return out.reshape(nb, t, D_MODEL), st

    assert x_prompt.shape[1] <= TOKEN_TILE, "a context sequence must fit one token tile (its conv has no halo)"
    s_zero = jnp.zeros((x_prompt.shape[0], 2, HEAD_DIM, A_WIDTH), F32)
    y_prompt, s_ctx = layer(x_prompt, mod_ctx, s_zero, False)
    y_sample, _ = layer(x_sample, mod_lat, to_state(state_wkv[:, 0]), True)
    new_state = from_state(s_ctx)[:, None]
    return (y_prompt, y_sample, new_state)
```

```python
import functools
import math

import numpy as np
import jax
import jax.numpy as jnp
from jax import lax
from jax.experimental import pallas as pl
from jax.experimental.pallas import tpu as pltpu

D_MODEL = 1024
A_WIDTH = 512
B_WIDTH = 512
HEAD_DIM = 64
A_HEADS = 8
GRID_W = 64
LORA = 64
IN_WIDTH = 4 * A_WIDTH + 4 * B_WIDTH
NORM_EPS = 1e-6
GN_EPS = 64e-5

CHUNK = 64
GROUP = 256
HEADS_PER_GROUP = GROUP // HEAD_DIM
N_GROUPS = A_WIDTH // GROUP
WKV_BATCH = 4
TOKEN_TILE = 512
POST_TILE = 1024
HALO = GRID_W
VMEM_LIMIT = 56 * 1024 * 1024

F32 = jnp.float32
BF16 = jnp.bfloat16


def _dg(a, b, nt=False):
    dims = (((1,), (1 if nt else 0,)), ((), ()))
    return lax.dot_general(a, b, dims, preferred_element_type=F32)


def _sigmoid(x):
    return 0.5 * jnp.tanh(0.5 * x) + 0.5


def _silu(x):
    h = 0.5 * x
    return h * jnp.tanh(h) + h


def _segsum(z, ones_ref):
    zb = z.astype(BF16)
    return jnp.concatenate([_dg(zb[:, g * GROUP:(g + 1) * GROUP], ones_ref[...]) for g in range(N_GROUPS)], axis=1)


def _adaln_kernel(c_ref, w_ref, b_ref, o_ref):
    c = c_ref[...]
    s = _silu(c)
    o_ref[...] = _dg(s.astype(BF16), w_ref[...].astype(BF16)) + b_ref[...]


def _adaln(cvec, w_ada, b_ada):
    rows = cvec.shape[0]
    return pl.pallas_call(
        _adaln_kernel,
        out_shape=jax.ShapeDtypeStruct((rows, 3 * D_MODEL), F32),
        compiler_params=pltpu.CompilerParams(vmem_limit_bytes=VMEM_LIMIT),
        name="adaln",
    )(cvec, w_ada, b_ada)


def _shift_rows(p, seg):
    rows = p.shape[0]
    assert seg & (seg - 1) == 0
    pos = lax.broadcasted_iota(jnp.int32, p.shape, 0) & (seg - 1)
    prev = jnp.where(pos == 0, 0.0, pltpu.roll(p, 1, 0))
    nxt = jnp.where(pos == seg - 1, 0.0, pltpu.roll(p, rows - 1, 0))
    return prev, nxt


def _pre_kernel(x_ref, mod_ref, g_ref, win_ref, wdn_ref, wdu_ref, wiu_ref, w0_ref, ab_ref, mu_ref,
                kks_ref, kas_ref, brk_ref, ones_ref,
                r_o, v_o, kk_o, lw0_o, lw1_o, kd0_o, kd1_o, b0_o, b1_o, bv_o, ga_o, cu_o, bg_o, *, seg):
    x = x_ref[0]
    mod = mod_ref[0]
    shift = mod[:, :D_MODEL]
    scale = mod[:, D_MODEL:2 * D_MODEL]
    gain = g_ref[...] * (1.0 + scale)
    h = x * lax.rsqrt(jnp.mean(x * x, axis=-1, keepdims=True) + NORM_EPS) * gain + shift
    hb = h.astype(BF16)

    def proj(i):
        return _dg(hb, win_ref[:, i * A_WIDTH:(i + 1) * A_WIDTH])

    def mixed(i):
        p = proj(i)
        prev, nxt = _shift_rows(p, seg)
        mu = mu_ref[i:i + 1, :]
        return p * (1.0 - mu) + (prev + nxt) * (0.5 * mu)

    low = _dg(hb, wdn_ref[...])
    lwz = jnp.tanh(low[:, :2 * LORA]).astype(BF16)
    la = low[:, 2 * LORA:].astype(BF16)
    wpre = _dg(lwz, wdu_ref[...]) + w0_ref[...]
    apre = _dg(la, wiu_ref[...]) + ab_ref[...]
    half_rate = -0.5 * math.exp(-0.5)
    lw = half_rate * jnp.tanh(0.5 * wpre) + half_rate
    a = _sigmoid(apre)
    lw0_o[0] = lw[:, :A_WIDTH]
    lw1_o[0] = lw[:, A_WIDTH:]

    r = mixed(0)
    k = mixed(1)
    v = mixed(2)
    r_o[0] = r
    v_o[0] = v

    ga = proj(3)
    ga_o[0] = _silu(ga).astype(BF16)
    bgate = proj(4)
    cgate = proj(5)
    uconv = proj(6)
    gb = proj(7)
    cu_o[0] = (cgate * uconv).astype(BF16)
    bg_o[0] = (bgate * _silu(gb)).astype(BF16)

    kks = k * kks_ref[...]
    ss = _segsum(kks * kks, ones_ref)
    kk = kks * lax.rsqrt(jnp.maximum(ss, 1e-24))
    kk_o[0] = kk
    kas = kas_ref[...]
    a0 = a[:, :A_WIDTH]
    a1 = a[:, A_WIDTH:]
    kd0 = k * (1.0 + (a0 - 1.0) * kas)
    kd1 = k * (1.0 + (a1 - 1.0) * kas)
    kd0_o[0] = kd0
    kd1_o[0] = kd1
    b0_o[0] = kk * a0
    b1_o[0] = kk * a1
    bonus = _segsum(r * (kd0 + kd1) * brk_ref[...], ones_ref)
    bv_o[0] = (bonus * v).astype(BF16)


def _const_spec(shape):
    nd = len(shape)
    return pl.BlockSpec(shape, lambda *_: (0,) * nd, pipeline_mode=pl.Buffered(1))


def _pre(x, mod, weights, seg):
    B, T, _ = x.shape
    tile = min(TOKEN_TILE, T)
    nt = T // tile
    tok = lambda w: pl.BlockSpec((1, tile, w), lambda b, t: (b, t, 0))
    mod_spec = pl.BlockSpec((1, 1, 3 * D_MODEL), (lambda b, t: (b, 0, 0)) if mod.shape[0] > 1 else (lambda b, t: (0, 0, 0)))
    n_f32, n_bf16 = 9, 4
    return pl.pallas_call(
        functools.partial(_pre_kernel, seg=seg),
        grid=(B, nt),
        in_specs=[tok(D_MODEL), mod_spec] + [_const_spec(w.shape) for w in weights],
        out_specs=[tok(A_WIDTH)] * (n_f32 + n_bf16),
        out_shape=[jax.ShapeDtypeStruct((B, T, A_WIDTH), F32)] * n_f32
        + [jax.ShapeDtypeStruct((B, T, A_WIDTH), BF16)] * n_bf16,
        compiler_params=pltpu.CompilerParams(dimension_semantics=("parallel", "parallel"),
                                             vmem_limit_bytes=VMEM_LIMIT),
        name="pre",
    )(x, mod, *weights)


def _wkv_streams(streams):
    L = CHUNK
    t = lax.broadcasted_iota(jnp.int32, (L, GROUP), 0)
    lane = lax.broadcasted_iota(jnp.int32, (L, GROUP), 1)
    s = lane & (L - 1)
    eye = (s == t).astype(F32)
    half = GROUP // 2
    lane_h = lax.broadcasted_iota(jnp.int32, (L, half), 1)
    half_masks = [((lane_h >> 6) == hh).astype(F32).astype(BF16) for hh in range(2)]
    zero_blk = jnp.zeros((2 * L, half), BF16)
    t2 = lax.broadcasted_iota(jnp.int32, (L, L), 0)
    s2 = lax.broadcasted_iota(jnp.int32, (L, L), 1)
    tri_f = (s2 <= t2).astype(F32).astype(BF16)
    tri_b = (s2 >= t2).astype(F32).astype(BF16)

    def bf(x):
        return x.astype(BF16)

    def bd(q):
        q_left, q_right = bf(q[:, :half]), bf(q[:, half:])
        left = jnp.concatenate([q_left * half_masks[0], q_left * half_masks[1], zero_blk], axis=0)
        right = jnp.concatenate([zero_blk, q_right * half_masks[0], q_right * half_masks[1]], axis=0)
        return jnp.concatenate([left, right], axis=1)

    def bdmm(parts, q, nt=False):
        rows = bf(parts[0]) if len(parts) == 1 else jnp.concatenate([bf(p) for p in parts], axis=0)
        out = _dg(rows, bd(q), nt)
        return [out[i * L:(i + 1) * L] for i in range(len(parts))]

    def each(fn, *cols):
        return [fn(*args) for args in zip(*cols)]

    r, kk, v, lw, kd, b, h0, bwd = (list(col) for col in zip(*streams))
    strict = [(s > t) if bw else (s < t) for bw in bwd]
    incl = [(s >= t) if bw else (s <= t) for bw in bwd]

    def cumsum(lw_, bw):
        tri = tri_b if bw else tri_f
        hi = bf(lw_)
        return _dg(tri, hi) + _dg(tri, bf(lw_ - hi.astype(F32)))

    c = each(cumsum, lw, bwd)
    c_last = each(lambda c_, bw: c_[0:1, :] if bw else c_[L - 1:L, :], c, bwd)
    e_neg = each(lambda c_: jnp.exp(-c_), c)
    kt = each(lambda kk_, c_, lw_: kk_ * jnp.exp(c_ - lw_), kk, c, lw)
    rt = each(lambda r_, c_: r_ * jnp.exp(c_), r, c)
    kdt = each(lambda kd_, e_: kd_ * e_, kd, e_neg)
    bt = each(lambda b_, e_: b_ * e_, b, e_neg)
    eye_g = each(lambda cl: eye * jnp.exp(cl), c_last)
    lhs1 = each(lambda a1, a2, a3: [a1, a2, a3], kt, rt, eye_g)
    gk = each(lambda p, q: bdmm(p, q, nt=True), lhs1, kdt)
    gb = each(lambda p, q: bdmm(p, q, nt=True), lhs1, bt)
    a_k = each(lambda m, g_: jnp.where(m, g_[0], 0.0), strict, gk)
    a_rk = each(lambda m, g_: jnp.where(m, g_[1], 0.0), incl, gk)
    a_b = each(lambda m, g_: jnp.where(m, g_[0], 0.0), strict, gb)
    a_rb = each(lambda m, g_: jnp.where(m, g_[1], 0.0), incl, gb)

    mm = lambda p, q: bdmm([p], q)[0]
    z = a_b
    m = 1
    while m < L:
        shift = m.bit_length() - 1
        join = ((t >> (shift + 1)) == (s >> (shift + 1))) & ((t >> shift) != (s >> shift))
        z = each(lambda z_: z_ - mm(jnp.where(join, z_, 0.0), z_), z)
        m *= 2

    gh = each(bdmm, lhs1, h0)
    gv = each(lambda ak, ark, g_, v_: bdmm([ak, ark, g_[2]], v_), a_k, a_rk, gk, v)
    rhs = each(lambda gh_, gv_: gh_[0] + gv_[0], gh, gv)
    u = each(lambda z_, rhs_: rhs_ - mm(z_, rhs_), z, rhs)
    gu = each(lambda arb, g_, u_: bdmm([arb, g_[2]], u_), a_rb, gb, u)
    y = each(lambda gh_, gv_, gu_: gh_[1] + gv_[1] - gu_[0], gh, gv, gu)
    h_next = each(lambda gh_, gv_, gu_: gh_[2] + gv_[2] - gu_[1], gh, gv, gu)
    return list(zip(y, h_next))


def _wkv_kernel(rf, kkf, vf, lwf, kdf, bf, rb, kkb, vb, lwb, kdb, bb, s0_ref, yf_ref, yb_ref, st_ref, h_ref):
    c = pl.program_id(1)

    @pl.when(c == 0)
    def _():
        h_ref[...] = s0_ref[...]

    dirs = ((rf, kkf, vf, lwf, kdf, bf, yf_ref), (rb, kkb, vb, lwb, kdb, bb, yb_ref))
    groups = [slice(g * GROUP, (g + 1) * GROUP) for g in range(N_GROUPS)]
    keys = [(i, d, sl) for i in range(WKV_BATCH) for d in range(2) for sl in groups]
    streams = [tuple(ref[i, :, sl] for ref in dirs[d][:6]) + (h_ref[i, d, :, sl], d == 1) for i, d, sl in keys]
    for (i, d, sl), (y, hn) in zip(keys, _wkv_streams(streams)):
        dirs[d][6][i, :, sl] = y.astype(BF16)
        h_ref[i, d, :, sl] = hn

    @pl.when(c == pl.num_programs(1) - 1)
    def _():
        st_ref[...] = h_ref[...]


def _wkv(r, kk, v, lw0, lw1, kd0, kd1, b0, b1, s0):
    B, T, _ = r.shape
    nc = T // CHUNK
    nb = WKV_BATCH
    assert B % nb == 0
    fwd = pl.BlockSpec((nb, CHUNK, A_WIDTH), lambda b, c: (b, c, 0))
    bwd = pl.BlockSpec((nb, CHUNK, A_WIDTH), lambda b, c: (b, nc - 1 - c, 0))
    st = pl.BlockSpec((nb, 2, HEAD_DIM, A_WIDTH), lambda b, c: (b, 0, 0, 0))
    st_in = st if s0.shape[0] == B else pl.BlockSpec((nb, 2, HEAD_DIM, A_WIDTH), lambda b, c: (0, 0, 0, 0))
    assert s0.shape[0] in (B, nb)
    return pl.pallas_call(
        _wkv_kernel,
        grid=(B // nb, nc),
        in_specs=[fwd] * 6 + [bwd] * 6 + [st_in],
        out_specs=[fwd, bwd, st],
        out_shape=[jax.ShapeDtypeStruct((B, T, A_WIDTH), BF16), jax.ShapeDtypeStruct((B, T, A_WIDTH), BF16),
                   jax.ShapeDtypeStruct((B, 2, HEAD_DIM, A_WIDTH), F32)],
        scratch_shapes=[pltpu.VMEM((nb, 2, HEAD_DIM, A_WIDTH), F32)],
        compiler_params=pltpu.CompilerParams(dimension_semantics=("parallel", "arbitrary"),
                                             vmem_limit_bytes=VMEM_LIMIT),
        name="wkv",
    )(r, kk, v, lw0, kd0, b0, r, kk, v, lw1, kd1, b1, s0)


def _post_kernel(*refs, latent, seq_len):
    if latent:
        (x_ref, mod_ref, yf_ref, yb_ref, bv_ref, ga_ref, cu_ref, cup_ref, cun_ref, bg_ref,
         gnw_ref, gnb_ref, cw_ref, wout_ref, fg_ref, ones_ref, o_ref) = refs
    else:
        (x_ref, mod_ref, yf_ref, yb_ref, bv_ref, ga_ref, cu_ref, bg_ref,
         gnw_ref, gnb_ref, cw_ref, wout_ref, fg_ref, ones_ref, o_ref) = refs
    x = x_ref[0]
    gate = mod_ref[0][:, 2 * D_MODEL:]
    y = yf_ref[0].astype(F32) + yb_ref[0].astype(F32)
    inv_n = 1.0 / HEAD_DIM
    mean = _segsum(y, ones_ref) * inv_n
    dlt = y - mean
    var = _segsum(dlt * dlt, ones_ref) * inv_n
    yn = dlt * lax.rsqrt(var + GN_EPS) * gnw_ref[...] + gnb_ref[...] + bv_ref[0].astype(F32)
    ya = yn * ga_ref[0].astype(F32)

    cu = cu_ref[0].astype(F32)
    w_prev, w_mid, w_next = cw_ref[0:1, :], cw_ref[1:2, :], cw_ref[2:3, :]
    if latent:
        half = B_WIDTH // 2
        t = pl.program_id(1)
        ph, nh = _shift_rows(cu[:, :half], GRID_W)
        top = jnp.where(t == 0, 0.0, cup_ref[0][:, half:].astype(F32))
        bot = jnp.where(t == pl.num_programs(1) - 1, 0.0, cun_ref[0][:, half:].astype(F32))
        pv = jnp.concatenate([top, cu[:cu.shape[0] - HALO, half:]], axis=0)
        nv = jnp.concatenate([cu[HALO:, half:], bot], axis=0)
        prev = jnp.concatenate([ph, pv], axis=1)
        nxt = jnp.concatenate([nh, nv], axis=1)
    else:
        prev, nxt = _shift_rows(cu, seq_len)
    yb = bg_ref[0].astype(F32) * (w_prev * prev + w_mid * cu + w_next * nxt)

    u = _dg(ya.astype(BF16), wout_ref[:A_WIDTH, :]) + _dg(yb.astype(BF16), wout_ref[A_WIDTH:, :])
    o = x + gate * u
    o_ref[0] = o * lax.rsqrt(jnp.mean(o * o, axis=-1, keepdims=True) + NORM_EPS) * fg_ref[...]


def _post(x, mod, yf, yb, bv, ga, cu, bg, weights, latent, seq_len):
    B, T, _ = x.shape
    tile = min(POST_TILE, T)
    nt = T // tile
    tok = lambda w: pl.BlockSpec((1, tile, w), lambda b, t: (b, t, 0))
    mod_spec = pl.BlockSpec((1, 1, 3 * D_MODEL), (lambda b, t: (b, 0, 0)) if mod.shape[0] > 1 else (lambda b, t: (0, 0, 0)))
    per_tile = tile // HALO
    n_halo = T // HALO
    halo_prev = pl.BlockSpec((1, HALO, B_WIDTH), lambda b, t: (b, jnp.maximum(t * per_tile - 1, 0), 0))
    halo_next = pl.BlockSpec((1, HALO, B_WIDTH), lambda b, t: (b, jnp.minimum((t + 1) * per_tile, n_halo - 1), 0))
    acts = [x, mod, yf, yb, bv, ga, cu] + ([cu, cu] if latent else []) + [bg]
    specs = [tok(D_MODEL), mod_spec] + [tok(A_WIDTH)] * 5 + ([halo_prev, halo_next] if latent else []) + [tok(B_WIDTH)]
    return pl.pallas_call(
        functools.partial(_post_kernel, latent=latent, seq_len=seq_len),
        grid=(B, nt),
        in_specs=specs + [_const_spec(w.shape) for w in weights],
        out_specs=tok(D_MODEL),
        out_shape=jax.ShapeDtypeStruct((B, T, D_MODEL), F32),
        compiler_params=pltpu.CompilerParams(dimension_semantics=("parallel", "parallel"),
                                             vmem_limit_bytes=VMEM_LIMIT),
        name="post",
    )(*acts, *weights)


def _block_diag2(w):
    z = jnp.zeros_like(w[0])
    return jnp.concatenate([jnp.concatenate([w[0], z], axis=1), jnp.concatenate([z, w[1]], axis=1)], axis=0)


def kernel(x_prompt, x_sample, c, state_wkv, c_ctx, w_ada, b_ada, norm_g, w_in, shift_mu, decay_w0, decay_down,
           decay_up, iclr_bias, iclr_down, iclr_up, kk_scale, ka_scale, bonus_rk, gn_w, gn_b, conv_w, w_out, final_g):
    assert w_in.shape[0] == 1, "single layer"
    n_lat = c.shape[0]
    row = lambda z: z.reshape(1, -1)

    pad = (-(n_lat + 1)) % 8
    cvec = jnp.concatenate([c, c_ctx[None, :], jnp.zeros((pad, D_MODEL), F32)], axis=0)
    mod = _adaln(cvec, w_ada[0], row(b_ada[0]))
    mod_lat = mod[:n_lat, None, :]
    mod_ctx = mod[n_lat:n_lat + 1, None, :]

    seg_id = np.arange(GROUP) // HEAD_DIM
    ones_bd = jnp.asarray((seg_id[:, None] == seg_id[None, :]).astype(np.float32), dtype=BF16)
    w_down = jnp.concatenate([decay_down[0, 0], decay_down[0, 1], iclr_down[0, 0], iclr_down[0, 1]], axis=1)
    pre_w = [row(norm_g[0]), w_in[0].astype(BF16), w_down.astype(BF16), _block_diag2(decay_up[0]).astype(BF16),
             _block_diag2(iclr_up[0]).astype(BF16), row(decay_w0[0]), row(iclr_bias[0]), shift_mu[0],
             row(kk_scale[0]), row(ka_scale[0]), row(bonus_rk[0]), ones_bd]
    post_w = [row(gn_w[0]), row(gn_b[0]), conv_w[0], w_out[0].astype(BF16), row(final_g), ones_bd]

    def to_state(s):
        return jnp.transpose(s, (0, 1, 4, 2, 3)).reshape(s.shape[0], 2, HEAD_DIM, A_WIDTH)

    def from_state(s):
        return jnp.transpose(s.reshape(s.shape[0], 2, HEAD_DIM, A_HEADS, HEAD_DIM), (0, 1, 3, 4, 2))

    def layer(x, mod_x, s0, latent):
        nb, t, _ = x.shape
        seg = GRID_W if latent else t
        fold = 1 if latent else math.gcd(nb, max(1, TOKEN_TILE // t))
        fold_in = lambda z: z.reshape(nb // fold, fold * t, z.shape[-1])
        outs = _pre(fold_in(x), mod_x, pre_w, seg)
        r, v, kk, lw0, lw1, kd0, kd1, b0, b1 = (z.reshape(nb, t, A_WIDTH) for z in outs[:9])
        bv, ga, cu, bg = outs[9:]
        yf, yb, st = _wkv(r, kk, v, lw0, lw1, kd0, kd1, b0, b1, s0)
        out = _post(fold_in(x), mod_x, fold_in(yf), fold_in(yb), bv, ga, cu, bg, post_w, latent, seg)
        return out.reshape(nb, t, D_MODEL), st

    assert x_prompt.shape[1] <= TOKEN_TILE, "a context sequence must fit one token tile (its conv has no halo)"
    s_zero = jnp.zeros((WKV_BATCH, 2, HEAD_DIM, A_WIDTH), F32)
    y_prompt, s_ctx = layer(x_prompt, mod_ctx, s_zero, False)
    y_sample, _ = layer(x_sample, mod_lat, to_state(state_wkv[:, 0]), True)
    new_state = from_state(s_ctx)[:, None]
    return (y_prompt, y_sample, new_state)
```

```python
import functools
import math

import numpy as np
import jax
import jax.numpy as jnp
from jax import lax
from jax.experimental import pallas as pl
from jax.experimental.pallas import tpu as pltpu

D_MODEL = 1024
A_WIDTH = 512
B_WIDTH = 512
HEAD_DIM = 64
A_HEADS = 8
GRID_W = 64
LORA = 64
IN_WIDTH = 4 * A_WIDTH + 4 * B_WIDTH
NORM_EPS = 1e-6
GN_EPS = 64e-5

CHUNK = 64
GROUP = 256
HEADS_PER_GROUP = GROUP // HEAD_DIM
N_GROUPS = A_WIDTH // GROUP
WKV_BATCH = 4
TOKEN_TILE = 512
POST_TILE = 1024
HALO = GRID_W
VMEM_LIMIT = 56 * 1024 * 1024

F32 = jnp.float32
BF16 = jnp.bfloat16


def _dg(a, b, nt=False):
    dims = (((1,), (1 if nt else 0,)), ((), ()))
    return lax.dot_general(a, b, dims, preferred_element_type=F32)


def _sigmoid(x):
    return 0.5 * jnp.tanh(0.5 * x) + 0.5


def _silu(x):
    h = 0.5 * x
    return h * jnp.tanh(h) + h


def _segsum(z, ones_ref):
    zb = z.astype(BF16)
    return jnp.concatenate([_dg(zb[:, g * GROUP:(g + 1) * GROUP], ones_ref[...]) for g in range(N_GROUPS)], axis=1)


def _adaln_kernel(c_ref, w_ref, b_ref, o_ref):
    c = c_ref[...]
    s = _silu(c)
    o_ref[...] = _dg(s.astype(BF16), w_ref[...].astype(BF16)) + b_ref[...]


def _adaln(cvec, w_ada, b_ada):
    rows = cvec.shape[0]
    return pl.pallas_call(
        _adaln_kernel,
        out_shape=jax.ShapeDtypeStruct((rows, 3 * D_MODEL), F32),
        compiler_params=pltpu.CompilerParams(vmem_limit_bytes=VMEM_LIMIT),
        name="adaln",
    )(cvec, w_ada, b_ada)


def _shift_rows(p, seg):
    rows = p.shape[0]
    assert seg & (seg - 1) == 0
    pos = lax.broadcasted_iota(jnp.int32, p.shape, 0) & (seg - 1)
    prev = jnp.where(pos == 0, 0.0, pltpu.roll(p, 1, 0))
    nxt = jnp.where(pos == seg - 1, 0.0, pltpu.roll(p, rows - 1, 0))
    return prev, nxt


def _pre_kernel(x_ref, mod_ref, g_ref, win_ref, wdn_ref, wdu_ref, wiu_ref, w0_ref, ab_ref, mu_ref,
                kks_ref, kas_ref, brk_ref, ones_ref,
                r_o, v_o, kk_o, lw0_o, lw1_o, kd0_o, kd1_o, b0_o, b1_o, bv_o, ga_o, cu_o, bg_o, *, seg):
    x = x_ref[0]
    mod = mod_ref[0]
    shift = mod[:, :D_MODEL]
    scale = mod[:, D_MODEL:2 * D_MODEL]
    gain = g_ref[...] * (1.0 + scale)
    h = x * lax.rsqrt(jnp.mean(x * x, axis=-1, keepdims=True) + NORM_EPS) * gain + shift
    hb = h.astype(BF16)

    def proj(i):
        return _dg(hb, win_ref[:, i * A_WIDTH:(i + 1) * A_WIDTH])

    def mixed(i):
        p = proj(i)
        prev, nxt = _shift_rows(p, seg)
        mu = mu_ref[i:i + 1, :]
        return p * (1.0 - mu) + (prev + nxt) * (0.5 * mu)

    low = _dg(hb, wdn_ref[...])
    lwz = jnp.tanh(low[:, :2 * LORA]).astype(BF16)
    la = low[:, 2 * LORA:].astype(BF16)
    wpre = _dg(lwz, wdu_ref[...]) + w0_ref[...]
    apre = _dg(la, wiu_ref[...]) + ab_ref[...]
    half_rate = -0.5 * math.exp(-0.5)
    lw = half_rate * jnp.tanh(0.5 * wpre) + half_rate
    a = _sigmoid(apre)
    lw0_o[0] = lw[:, :A_WIDTH]
    lw1_o[0] = lw[:, A_WIDTH:]

    r = mixed(0)
    k = mixed(1)
    v = mixed(2)
    r_o[0] = r
    v_o[0] = v

    ga = proj(3)
    ga_o[0] = _silu(ga).astype(BF16)
    bgate = proj(4)
    cgate = proj(5)
    uconv = proj(6)
    gb = proj(7)
    cu_o[0] = (cgate * uconv).astype(BF16)
    bg_o[0] = (bgate * _silu(gb)).astype(BF16)

    kks = k * kks_ref[...]
    ss = _segsum(kks * kks, ones_ref)
    kk = kks * lax.rsqrt(jnp.maximum(ss, 1e-24))
    kk_o[0] = kk
    kas = kas_ref[...]
    a0 = a[:, :A_WIDTH]
    a1 = a[:, A_WIDTH:]
    kd0 = k * (1.0 + (a0 - 1.0) * kas)
    kd1 = k * (1.0 + (a1 - 1.0) * kas)
    kd0_o[0] = kd0
    kd1_o[0] = kd1
    b0_o[0] = kk * a0
    b1_o[0] = kk * a1
    bonus = _segsum(r * (kd0 + kd1) * brk_ref[...], ones_ref)
    bv_o[0] = (bonus * v).astype(BF16)


def _const_spec(shape):
    nd = len(shape)
    return pl.BlockSpec(shape, lambda *_: (0,) * nd, pipeline_mode=pl.Buffered(1))


def _pre(x, mod, weights, seg):
    B, T, _ = x.shape
    tile = min(TOKEN_TILE, T)
    nt = T // tile
    tok = lambda w: pl.BlockSpec((1, tile, w), lambda b, t: (b, t, 0))
    mod_spec = pl.BlockSpec((1, 1, 3 * D_MODEL), (lambda b, t: (b, 0, 0)) if mod.shape[0] > 1 else (lambda b, t: (0, 0, 0)))
    n_f32, n_bf16 = 9, 4
    return pl.pallas_call(
        functools.partial(_pre_kernel, seg=seg),
        grid=(B, nt),
        in_specs=[tok(D_MODEL), mod_spec] + [_const_spec(w.shape) for w in weights],
        out_specs=[tok(A_WIDTH)] * (n_f32 + n_bf16),
        out_shape=[jax.ShapeDtypeStruct((B, T, A_WIDTH), F32)] * n_f32
        + [jax.ShapeDtypeStruct((B, T, A_WIDTH), BF16)] * n_bf16,
        compiler_params=pltpu.CompilerParams(dimension_semantics=("parallel", "parallel"),
                                             vmem_limit_bytes=VMEM_LIMIT),
        name="pre",
    )(x, mod, *weights)


def _wkv_streams(streams):
    L = CHUNK
    t = lax.broadcasted_iota(jnp.int32, (L, GROUP), 0)
    lane = lax.broadcasted_iota(jnp.int32, (L, GROUP), 1)
    s = lane & (L - 1)
    eye = (s == t).astype(F32)
    half = GROUP // 2
    lane_h = lax.broadcasted_iota(jnp.int32, (L, half), 1)
    half_masks = [((lane_h >> 6) == hh).astype(F32).astype(BF16) for hh in range(2)]
    zero_blk = jnp.zeros((2 * L, half), BF16)
    t2 = lax.broadcasted_iota(jnp.int32, (L, L), 0)
    s2 = lax.broadcasted_iota(jnp.int32, (L, L), 1)
    tri_f = (s2 <= t2).astype(F32).astype(BF16)
    tri_b = (s2 >= t2).astype(F32).astype(BF16)

    def bf(x):
        return x.astype(BF16)

    def bd(q):
        q_left, q_right = bf(q[:, :half]), bf(q[:, half:])
        left = jnp.concatenate([q_left * half_masks[0], q_left * half_masks[1], zero_blk], axis=0)
        right = jnp.concatenate([zero_blk, q_right * half_masks[0], q_right * half_masks[1]], axis=0)
        return jnp.concatenate([left, right], axis=1)

    def bdmm(parts, q, nt=False):
        rows = bf(parts[0]) if len(parts) == 1 else jnp.concatenate([bf(p) for p in parts], axis=0)
        out = _dg(rows, bd(q), nt)
        return [out[i * L:(i + 1) * L] for i in range(len(parts))]

    def each(fn, *cols):
        return [fn(*args) for args in zip(*cols)]

    r, kk, v, lw, kd, b, h0, bwd = (list(col) for col in zip(*streams))
    strict = [(s > t) if bw else (s < t) for bw in bwd]
    incl = [(s >= t) if bw else (s <= t) for bw in bwd]

    def cumsum(lw_, bw):
        tri = tri_b if bw else tri_f
        hi = bf(lw_)
        return _dg(tri, hi) + _dg(tri, bf(lw_ - hi.astype(F32)))

    c = each(cumsum, lw, bwd)
    c_last = each(lambda c_, bw: c_[0:1, :] if bw else c_[L - 1:L, :], c, bwd)
    e_neg = each(lambda c_: jnp.exp(-c_), c)
    kt = each(lambda kk_, c_, lw_: kk_ * jnp.exp(c_ - lw_), kk, c, lw)
    rt = each(lambda r_, c_: r_ * jnp.exp(c_), r, c)
    kdt = each(lambda kd_, e_: kd_ * e_, kd, e_neg)
    bt = each(lambda b_, e_: b_ * e_, b, e_neg)
    eye_g = each(lambda cl: eye * jnp.exp(cl), c_last)
    lhs1 = each(lambda a1, a2, a3: [a1, a2, a3], kt, rt, eye_g)
    gk = each(lambda p, q: bdmm(p, q, nt=True), lhs1, kdt)
    gb = each(lambda p, q: bdmm(p, q, nt=True), lhs1, bt)
    a_k = each(lambda m, g_: jnp.where(m, g_[0], 0.0), strict, gk)
    a_rk = each(lambda m, g_: jnp.where(m, g_[1], 0.0), incl, gk)
    a_b = each(lambda m, g_: jnp.where(m, g_[0], 0.0), strict, gb)
    a_rb = each(lambda m, g_: jnp.where(m, g_[1], 0.0), incl, gb)

    def mm(p, q):
        p_hi = bf(p)
        p_lo, q_lo = p - p_hi.astype(F32), q - bf(q).astype(F32)
        main = _dg(jnp.concatenate([p_hi, bf(p_lo)], axis=0), bd(q))
        return main[:L] + main[L:] + _dg(p_hi, bd(q_lo))

    z = a_b
    m = 1
    while m < L:
        shift = m.bit_length() - 1
        join = ((t >> (shift + 1)) == (s >> (shift + 1))) & ((t >> shift) != (s >> shift))
        z = each(lambda z_: z_ - mm(jnp.where(join, z_, 0.0), z_), z)
        m *= 2

    def state_products(parts, h_):
        kt_, rt_, eg = parts
        eg_hi = bf(eg)
        rows = jnp.concatenate([bf(kt_), bf(rt_), eg_hi, bf(eg - eg_hi.astype(F32))], axis=0)
        main = _dg(rows, bd(h_))
        carry = main[2 * L:3 * L] + main[3 * L:] + _dg(eg_hi, bd(h_ - bf(h_).astype(F32)))
        return [main[:L], main[L:2 * L], carry]

    gh = each(state_products, lhs1, h0)
    gv = each(lambda ak, ark, g_, v_: bdmm([ak, ark, g_[2]], v_), a_k, a_rk, gk, v)
    rhs = each(lambda gh_, gv_: gh_[0] + gv_[0], gh, gv)
    u = each(lambda z_, rhs_: rhs_ - mm(z_, rhs_), z, rhs)
    gu = each(lambda arb, g_, u_: bdmm([arb, g_[2]], u_), a_rb, gb, u)
    y = each(lambda gh_, gv_, gu_: gh_[1] + gv_[1] - gu_[0], gh, gv, gu)
    h_next = each(lambda gh_, gv_, gu_: gh_[2] + gv_[2] - gu_[1], gh, gv, gu)
    return list(zip(y, h_next))


def _wkv_kernel(rf, kkf, vf, lwf, kdf, bf, rb, kkb, vb, lwb, kdb, bb, s0_ref, yf_ref, yb_ref, st_ref, h_ref):
    c = pl.program_id(1)

    @pl.when(c == 0)
    def _():
        h_ref[...] = s0_ref[...]

    dirs = ((rf, kkf, vf, lwf, kdf, bf, yf_ref), (rb, kkb, vb, lwb, kdb, bb, yb_ref))
    groups = [slice(g * GROUP, (g + 1) * GROUP) for g in range(N_GROUPS)]
    keys = [(i, d, sl) for i in range(WKV_BATCH) for d in range(2) for sl in groups]
    streams = [tuple(ref[i, :, sl] for ref in dirs[d][:6]) + (h_ref[i, d, :, sl], d == 1) for i, d, sl in keys]
    for (i, d, sl), (y, hn) in zip(keys, _wkv_streams(streams)):
        dirs[d][6][i, :, sl] = y.astype(BF16)
        h_ref[i, d, :, sl] = hn

    @pl.when(c == pl.num_programs(1) - 1)
    def _():
        st_ref[...] = h_ref[...]


def _wkv(r, kk, v, lw0, lw1, kd0, kd1, b0, b1, s0):
    B, T, _ = r.shape
    nc = T // CHUNK
    nb = WKV_BATCH
    assert B % nb == 0
    fwd = pl.BlockSpec((nb, CHUNK, A_WIDTH), lambda b, c: (b, c, 0))
    bwd = pl.BlockSpec((nb, CHUNK, A_WIDTH), lambda b, c: (b, nc - 1 - c, 0))
    st = pl.BlockSpec((nb, 2, HEAD_DIM, A_WIDTH), lambda b, c: (b, 0, 0, 0))
    st_in = st if s0.shape[0] == B else pl.BlockSpec((nb, 2, HEAD_DIM, A_WIDTH), lambda b, c: (0, 0, 0, 0))
    assert s0.shape[0] in (B, nb)
    return pl.pallas_call(
        _wkv_kernel,
        grid=(B // nb, nc),
        in_specs=[fwd] * 6 + [bwd] * 6 + [st_in],
        out_specs=[fwd, bwd, st],
        out_shape=[jax.ShapeDtypeStruct((B, T, A_WIDTH), BF16), jax.ShapeDtypeStruct((B, T, A_WIDTH), BF16),
                   jax.ShapeDtypeStruct((B, 2, HEAD_DIM, A_WIDTH), F32)],
        scratch_shapes=[pltpu.VMEM((nb, 2, HEAD_DIM, A_WIDTH), F32)],
        compiler_params=pltpu.CompilerParams(dimension_semantics=("parallel", "arbitrary"),
                                             vmem_limit_bytes=VMEM_LIMIT),
        name="wkv",
    )(r, kk, v, lw0, kd0, b0, r, kk, v, lw1, kd1, b1, s0)


def _post_kernel(*refs, latent, seq_len):
    if latent:
        (x_ref, mod_ref, yf_ref, yb_ref, bv_ref, ga_ref, cu_ref, cup_ref, cun_ref, bg_ref,
         gnw_ref, gnb_ref, cw_ref, wout_ref, fg_ref, ones_ref, o_ref) = refs
    else:
        (x_ref, mod_ref, yf_ref, yb_ref, bv_ref, ga_ref, cu_ref, bg_ref,
         gnw_ref, gnb_ref, cw_ref, wout_ref, fg_ref, ones_ref, o_ref) = refs
    x = x_ref[0]
    gate = mod_ref[0][:, 2 * D_MODEL:]
    y = yf_ref[0].astype(F32) + yb_ref[0].astype(F32)
    inv_n = 1.0 / HEAD_DIM
    mean = _segsum(y, ones_ref) * inv_n
    dlt = y - mean
    var = _segsum(dlt * dlt, ones_ref) * inv_n
    yn = dlt * lax.rsqrt(var + GN_EPS) * gnw_ref[...] + gnb_ref[...] + bv_ref[0].astype(F32)
    ya = yn * ga_ref[0].astype(F32)

    cu = cu_ref[0].astype(F32)
    w_prev, w_mid, w_next = cw_ref[0:1, :], cw_ref[1:2, :], cw_ref[2:3, :]
    if latent:
        half = B_WIDTH // 2
        t = pl.program_id(1)
        ph, nh = _shift_rows(cu[:, :half], GRID_W)
        top = jnp.where(t == 0, 0.0, cup_ref[0][:, half:].astype(F32))
        bot = jnp.where(t == pl.num_programs(1) - 1, 0.0, cun_ref[0][:, half:].astype(F32))
        pv = jnp.concatenate([top, cu[:cu.shape[0] - HALO, half:]], axis=0)
        nv = jnp.concatenate([cu[HALO:, half:], bot], axis=0)
        prev = jnp.concatenate([ph, pv], axis=1)
        nxt = jnp.concatenate([nh, nv], axis=1)
    else:
        prev, nxt = _shift_rows(cu, seq_len)
    yb = bg_ref[0].astype(F32) * (w_prev * prev + w_mid * cu + w_next * nxt)

    u = _dg(ya.astype(BF16), wout_ref[:A_WIDTH, :]) + _dg(yb.astype(BF16), wout_ref[A_WIDTH:, :])
    o = x + gate * u
    o_ref[0] = o * lax.rsqrt(jnp.mean(o * o, axis=-1, keepdims=True) + NORM_EPS) * fg_ref[...]


def _post(x, mod, yf, yb, bv, ga, cu, bg, weights, latent, seq_len):
    B, T, _ = x.shape
    tile = min(POST_TILE, T)
    nt = T // tile
    tok = lambda w: pl.BlockSpec((1, tile, w), lambda b, t: (b, t, 0))
    mod_spec = pl.BlockSpec((1, 1, 3 * D_MODEL), (lambda b, t: (b, 0, 0)) if mod.shape[0] > 1 else (lambda b, t: (0, 0, 0)))
    per_tile = tile // HALO
    n_halo = T // HALO
    halo_prev = pl.BlockSpec((1, HALO, B_WIDTH), lambda b, t: (b, jnp.maximum(t * per_tile - 1, 0), 0))
    halo_next = pl.BlockSpec((1, HALO, B_WIDTH), lambda b, t: (b, jnp.minimum((t + 1) * per_tile, n_halo - 1), 0))
    acts = [x, mod, yf, yb, bv, ga, cu] + ([cu, cu] if latent else []) + [bg]
    specs = [tok(D_MODEL), mod_spec] + [tok(A_WIDTH)] * 5 + ([halo_prev, halo_next] if latent else []) + [tok(B_WIDTH)]
    return pl.pallas_call(
        functools.partial(_post_kernel, latent=latent, seq_len=seq_len),
        grid=(B, nt),
        in_specs=specs + [_const_spec(w.shape) for w in weights],
        out_specs=tok(D_MODEL),
        out_shape=jax.ShapeDtypeStruct((B, T, D_MODEL), F32),
        compiler_params=pltpu.CompilerParams(dimension_semantics=("parallel", "parallel"),
                                             vmem_limit_bytes=VMEM_LIMIT),
        name="post",
    )(*acts, *weights)


def _block_diag2(w):
    z = jnp.zeros_like(w[0])
    return jnp.concatenate([jnp.concatenate([w[0], z], axis=1), jnp.concatenate([z, w[1]], axis=1)], axis=0)


def kernel(x_prompt, x_sample, c, state_wkv, c_ctx, w_ada, b_ada, norm_g, w_in, shift_mu, decay_w0, decay_down,
           decay_up, iclr_bias, iclr_down, iclr_up, kk_scale, ka_scale, bonus_rk, gn_w, gn_b, conv_w, w_out, final_g):
    assert w_in.shape[0] == 1, "single layer"
    n_lat = c.shape[0]
    row = lambda z: z.reshape(1, -1)

    pad = (-(n_lat + 1)) % 8
    cvec = jnp.concatenate([c, c_ctx[None, :], jnp.zeros((pad, D_MODEL), F32)], axis=0)
    mod = _adaln(cvec, w_ada[0], row(b_ada[0]))
    mod_lat = mod[:n_lat, None, :]
    mod_ctx = mod[n_lat:n_lat + 1, None, :]

    seg_id = np.arange(GROUP) // HEAD_DIM
    ones_bd = jnp.asarray((seg_id[:, None] == seg_id[None, :]).astype(np.float32), dtype=BF16)
    w_down = jnp.concatenate([decay_down[0, 0], decay_down[0, 1], iclr_down[0, 0], iclr_down[0, 1]], axis=1)
    pre_w = [row(norm_g[0]), w_in[0].astype(BF16), w_down.astype(BF16), _block_diag2(decay_up[0]).astype(BF16),
             _block_diag2(iclr_up[0]).astype(BF16), row(decay_w0[0]), row(iclr_bias[0]), shift_mu[0],
             row(kk_scale[0]), row(ka_scale[0]), row(bonus_rk[0]), ones_bd]
    post_w = [row(gn_w[0]), row(gn_b[0]), conv_w[0], w_out[0].astype(BF16), row(final_g), ones_bd]

    def to_state(s):
        return jnp.transpose(s, (0, 1, 4, 2, 3)).reshape(s.shape[0], 2, HEAD_DIM, A_WIDTH)

    def from_state(s):
        return jnp.transpose(s.reshape(s.shape[0], 2, HEAD_DIM, A_HEADS, HEAD_DIM), (0, 1, 3, 4, 2))

    def layer(x, mod_x, s0, latent):
        nb, t, _ = x.shape
        seg = GRID_W if latent else t
        fold = 1 if latent else math.gcd(nb, max(1, TOKEN_TILE // t))
        fold_in = lambda z: z.reshape(nb // fold, fold * t, z.shape[-1])
        outs = _pre(fold_in(x), mod_x, pre_w, seg)
        r, v, kk, lw0, lw1, kd0, kd1, b0, b1 = (z.reshape(nb, t, A_WIDTH) for z in outs[:9])
        bv, ga, cu, bg = outs[9:]
        yf, yb, st = _wkv(r, kk, v, lw0, lw1, kd0, kd1, b0, b1, s0)
        out = _post(fold_in(x), mod_x, fold_in(yf), fold_in(yb), bv, ga, cu, bg, post_w, latent, seg)
        return out.reshape(nb, t, D_MODEL), st

    assert x_prompt.shape[1] <= TOKEN_TILE, "a context sequence must fit one token tile (its conv has no halo)"
    s_zero = jnp.zeros((WKV_BATCH, 2, HEAD_DIM, A_WIDTH), F32)
    y_prompt, s_ctx = layer(x_prompt, mod_ctx, s_zero, False)
    y_sample, _ = layer(x_sample, mod_lat, to_state(state_wkv[:, 0]), True)
    new_state = from_state(s_ctx)[:, None]
    return (y_prompt, y_sample, new_state)
```

```python
import functools
import math

import numpy as np
import jax
import jax.numpy as jnp
from jax import lax
from jax.experimental import pallas as pl
from jax.experimental.pallas import tpu as pltpu

D_MODEL = 1024
A_WIDTH = 512
B_WIDTH = 512
HEAD_DIM = 64
A_HEADS = 8
GRID_W = 64
LORA = 64
IN_WIDTH = 4 * A_WIDTH + 4 * B_WIDTH
NORM_EPS = 1e-6
GN_EPS = 64e-5

CHUNK = 64
GROUP = 256
HEADS_PER_GROUP = GROUP // HEAD_DIM
N_GROUPS = A_WIDTH // GROUP
WKV_BATCH = 4
TOKEN_TILE = 512
POST_TILE = 1024
HALO = GRID_W
VMEM_LIMIT = 56 * 1024 * 1024

F32 = jnp.float32
BF16 = jnp.bfloat16


def _dg(a, b, nt=False):
    dims = (((1,), (1 if nt else 0,)), ((), ()))
    return lax.dot_general(a, b, dims, preferred_element_type=F32)


def _sigmoid(x):
    return 0.5 * jnp.tanh(0.5 * x) + 0.5


def _silu(x):
    h = 0.5 * x
    return h * jnp.tanh(h) + h


def _segsum(z, ones_ref):
    zb = z.astype(BF16)
    return jnp.concatenate([_dg(zb[:, g * GROUP:(g + 1) * GROUP], ones_ref[...]) for g in range(N_GROUPS)], axis=1)


def _adaln_kernel(c_ref, w_ref, b_ref, o_ref):
    c = c_ref[...]
    s = _silu(c)
    o_ref[...] = _dg(s.astype(BF16), w_ref[...].astype(BF16)) + b_ref[...]


def _adaln(cvec, w_ada, b_ada):
    rows = cvec.shape[0]
    return pl.pallas_call(
        _adaln_kernel,
        out_shape=jax.ShapeDtypeStruct((rows, 3 * D_MODEL), F32),
        compiler_params=pltpu.CompilerParams(vmem_limit_bytes=VMEM_LIMIT),
        name="adaln",
    )(cvec, w_ada, b_ada)


def _shift_rows(p, seg):
    rows = p.shape[0]
    assert seg & (seg - 1) == 0
    pos = lax.broadcasted_iota(jnp.int32, p.shape, 0) & (seg - 1)
    prev = jnp.where(pos == 0, 0.0, pltpu.roll(p, 1, 0))
    nxt = jnp.where(pos == seg - 1, 0.0, pltpu.roll(p, rows - 1, 0))
    return prev, nxt


def _pre_kernel(x_ref, mod_ref, g_ref, win_ref, wdn_ref, wdu_ref, wiu_ref, w0_ref, ab_ref, mu_ref,
                kks_ref, kas_ref, brk_ref, ones_ref,
                r_o, v_o, kk_o, lw0_o, lw1_o, kd0_o, kd1_o, b0_o, b1_o, bv_o, ga_o, cu_o, bg_o, *, seg):
    x = x_ref[0]
    mod = mod_ref[0]
    shift = mod[:, :D_MODEL]
    scale = mod[:, D_MODEL:2 * D_MODEL]
    gain = g_ref[...] * (1.0 + scale)
    h = x * lax.rsqrt(jnp.mean(x * x, axis=-1, keepdims=True) + NORM_EPS) * gain + shift
    hb = h.astype(BF16)

    def proj(i):
        return _dg(hb, win_ref[:, i * A_WIDTH:(i + 1) * A_WIDTH])

    def mixed(i):
        p = proj(i)
        prev, nxt = _shift_rows(p, seg)
        mu = mu_ref[i:i + 1, :]
        return p * (1.0 - mu) + (prev + nxt) * (0.5 * mu)

    low = _dg(hb, wdn_ref[...])
    lwz = jnp.tanh(low[:, :2 * LORA]).astype(BF16)
    la = low[:, 2 * LORA:].astype(BF16)
    wpre = _dg(lwz, wdu_ref[...]) + w0_ref[...]
    apre = _dg(la, wiu_ref[...]) + ab_ref[...]
    half_rate = -0.5 * math.exp(-0.5)
    lw = half_rate * jnp.tanh(0.5 * wpre) + half_rate
    a = _sigmoid(apre)
    lw0_o[0] = lw[:, :A_WIDTH]
    lw1_o[0] = lw[:, A_WIDTH:]

    r = mixed(0)
    k = mixed(1)
    v = mixed(2)
    r_o[0] = r
    v_o[0] = v

    ga = proj(3)
    ga_o[0] = _silu(ga).astype(BF16)
    bgate = proj(4)
    cgate = proj(5)
    uconv = proj(6)
    gb = proj(7)
    cu_o[0] = (cgate * uconv).astype(BF16)
    bg_o[0] = (bgate * _silu(gb)).astype(BF16)

    kks = k * kks_ref[...]
    ss = _segsum(kks * kks, ones_ref)
    kk = kks * lax.rsqrt(jnp.maximum(ss, 1e-24))
    kk_o[0] = kk
    kas = kas_ref[...]
    a0 = a[:, :A_WIDTH]
    a1 = a[:, A_WIDTH:]
    kd0 = k * (1.0 + (a0 - 1.0) * kas)
    kd1 = k * (1.0 + (a1 - 1.0) * kas)
    kd0_o[0] = kd0
    kd1_o[0] = kd1
    b0_o[0] = kk * a0
    b1_o[0] = kk * a1
    bonus = _segsum(r * (kd0 + kd1) * brk_ref[...], ones_ref)
    bv_o[0] = (bonus * v).astype(BF16)


def _const_spec(shape):
    nd = len(shape)
    return pl.BlockSpec(shape, lambda *_: (0,) * nd, pipeline_mode=pl.Buffered(1))


def _pre(x, mod, weights, seg):
    B, T, _ = x.shape
    tile = min(TOKEN_TILE, T)
    nt = T // tile
    tok = lambda w: pl.BlockSpec((1, tile, w), lambda b, t: (b, t, 0))
    mod_spec = pl.BlockSpec((1, 1, 3 * D_MODEL), (lambda b, t: (b, 0, 0)) if mod.shape[0] > 1 else (lambda b, t: (0, 0, 0)))
    n_f32, n_bf16 = 9, 4
    return pl.pallas_call(
        functools.partial(_pre_kernel, seg=seg),
        grid=(B, nt),
        in_specs=[tok(D_MODEL), mod_spec] + [_const_spec(w.shape) for w in weights],
        out_specs=[tok(A_WIDTH)] * (n_f32 + n_bf16),
        out_shape=[jax.ShapeDtypeStruct((B, T, A_WIDTH), F32)] * n_f32
        + [jax.ShapeDtypeStruct((B, T, A_WIDTH), BF16)] * n_bf16,
        compiler_params=pltpu.CompilerParams(dimension_semantics=("parallel", "parallel"),
                                             vmem_limit_bytes=VMEM_LIMIT),
        name="pre",
    )(x, mod, *weights)


def _wkv_streams(streams):
    L = CHUNK
    t = lax.broadcasted_iota(jnp.int32, (L, GROUP), 0)
    lane = lax.broadcasted_iota(jnp.int32, (L, GROUP), 1)
    s = lane & (L - 1)
    eye = (s == t).astype(F32)
    half = GROUP // 2
    lane_h = lax.broadcasted_iota(jnp.int32, (L, half), 1)
    half_masks = [((lane_h >> 6) == hh).astype(F32).astype(BF16) for hh in range(2)]
    zero_blk = jnp.zeros((2 * L, half), BF16)
    t2 = lax.broadcasted_iota(jnp.int32, (L, L), 0)
    s2 = lax.broadcasted_iota(jnp.int32, (L, L), 1)
    tri_f = (s2 <= t2).astype(F32).astype(BF16)
    tri_b = (s2 >= t2).astype(F32).astype(BF16)

    def bf(x):
        return x.astype(BF16)

    def bd(q):
        q_left, q_right = bf(q[:, :half]), bf(q[:, half:])
        left = jnp.concatenate([q_left * half_masks[0], q_left * half_masks[1], zero_blk], axis=0)
        right = jnp.concatenate([zero_blk, q_right * half_masks[0], q_right * half_masks[1]], axis=0)
        return jnp.concatenate([left, right], axis=1)

    def bdmm(parts, q, nt=False):
        rows = bf(parts[0]) if len(parts) == 1 else jnp.concatenate([bf(p) for p in parts], axis=0)
        out = _dg(rows, bd(q), nt)
        return [out[i * L:(i + 1) * L] for i in range(len(parts))]

    def each(fn, *cols):
        return [fn(*args) for args in zip(*cols)]

    r, kk, v, lw, kd, b, h0, bwd = (list(col) for col in zip(*streams))
    strict = [(s > t) if bw else (s < t) for bw in bwd]
    incl = [(s >= t) if bw else (s <= t) for bw in bwd]

    def cumsum(lw_, bw):
        tri = tri_b if bw else tri_f
        hi = bf(lw_)
        return _dg(tri, hi) + _dg(tri, bf(lw_ - hi.astype(F32)))

    c = each(cumsum, lw, bwd)
    c_last = each(lambda c_, bw: c_[0:1, :] if bw else c_[L - 1:L, :], c, bwd)
    e_neg = each(lambda c_: jnp.exp(-c_), c)
    kt = each(lambda kk_, c_, lw_: kk_ * jnp.exp(c_ - lw_), kk, c, lw)
    rt = each(lambda r_, c_: r_ * jnp.exp(c_), r, c)
    kdt = each(lambda kd_, e_: kd_ * e_, kd, e_neg)
    bt = each(lambda b_, e_: b_ * e_, b, e_neg)
    eye_g = each(lambda cl: eye * jnp.exp(cl), c_last)
    lhs1 = each(lambda a1, a2, a3: [a1, a2, a3], kt, rt, eye_g)
    gk = each(lambda p, q: bdmm(p, q, nt=True), lhs1, kdt)
    gb = each(lambda p, q: bdmm(p, q, nt=True), lhs1, bt)
    a_k = each(lambda m, g_: jnp.where(m, g_[0], 0.0), strict, gk)
    a_rk = each(lambda m, g_: jnp.where(m, g_[1], 0.0), incl, gk)
    a_b = each(lambda m, g_: jnp.where(m, g_[0], 0.0), strict, gb)
    a_rb = each(lambda m, g_: jnp.where(m, g_[1], 0.0), incl, gb)

    def mm(p, q):
        p_hi = bf(p)
        p_lo, q_lo = p - p_hi.astype(F32), q - bf(q).astype(F32)
        main = _dg(jnp.concatenate([p_hi, bf(p_lo)], axis=0), bd(q))
        return main[:L] + main[L:] + _dg(p_hi, bd(q_lo))

    mm1 = lambda p, q: bdmm([p], q)[0]
    z = a_b
    m = 1
    while m < L:
        shift = m.bit_length() - 1
        join = ((t >> (shift + 1)) == (s >> (shift + 1))) & ((t >> shift) != (s >> shift))
        z = each(lambda z_: z_ - mm1(jnp.where(join, z_, 0.0), z_), z)
        m *= 2

    def state_products(parts, h_):
        kt_, rt_, eg = parts
        eg_hi = bf(eg)
        rows = jnp.concatenate([bf(kt_), bf(rt_), eg_hi, bf(eg - eg_hi.astype(F32))], axis=0)
        main = _dg(rows, bd(h_))
        carry = main[2 * L:3 * L] + main[3 * L:] + _dg(eg_hi, bd(h_ - bf(h_).astype(F32)))
        return [main[:L], main[L:2 * L], carry]

    gh = each(state_products, lhs1, h0)
    gv = each(lambda ak, ark, g_, v_: bdmm([ak, ark, g_[2]], v_), a_k, a_rk, gk, v)
    rhs = each(lambda gh_, gv_: gh_[0] + gv_[0], gh, gv)
    u0 = each(lambda z_, rhs_: rhs_ - mm1(z_, rhs_), z, rhs)
    resid = each(lambda rhs_, u_, a: rhs_ - u_ - mm(a, u_), rhs, u0, a_b)
    u = each(lambda u_, z_, r_: u_ + r_ - mm1(z_, r_), u0, z, resid)
    gu = each(lambda arb, g_, u_: bdmm([arb, g_[2]], u_), a_rb, gb, u)
    y = each(lambda gh_, gv_, gu_: gh_[1] + gv_[1] - gu_[0], gh, gv, gu)
    h_next = each(lambda gh_, gv_, gu_: gh_[2] + gv_[2] - gu_[1], gh, gv, gu)
    return list(zip(y, h_next))


def _wkv_kernel(rf, kkf, vf, lwf, kdf, bf, rb, kkb, vb, lwb, kdb, bb, s0_ref, yf_ref, yb_ref, st_ref, h_ref):
    c = pl.program_id(1)

    @pl.when(c == 0)
    def _():
        h_ref[...] = s0_ref[...]

    dirs = ((rf, kkf, vf, lwf, kdf, bf, yf_ref), (rb, kkb, vb, lwb, kdb, bb, yb_ref))
    groups = [slice(g * GROUP, (g + 1) * GROUP) for g in range(N_GROUPS)]
    keys = [(i, d, sl) for i in range(WKV_BATCH) for d in range(2) for sl in groups]
    streams = [tuple(ref[i, :, sl] for ref in dirs[d][:6]) + (h_ref[i, d, :, sl], d == 1) for i, d, sl in keys]
    for (i, d, sl), (y, hn) in zip(keys, _wkv_streams(streams)):
        dirs[d][6][i, :, sl] = y.astype(BF16)
        h_ref[i, d, :, sl] = hn

    @pl.when(c == pl.num_programs(1) - 1)
    def _():
        st_ref[...] = h_ref[...]


def _wkv(r, kk, v, lw0, lw1, kd0, kd1, b0, b1, s0):
    B, T, _ = r.shape
    nc = T // CHUNK
    nb = WKV_BATCH
    assert B % nb == 0
    fwd = pl.BlockSpec((nb, CHUNK, A_WIDTH), lambda b, c: (b, c, 0))
    bwd = pl.BlockSpec((nb, CHUNK, A_WIDTH), lambda b, c: (b, nc - 1 - c, 0))
    st = pl.BlockSpec((nb, 2, HEAD_DIM, A_WIDTH), lambda b, c: (b, 0, 0, 0))
    st_in = st if s0.shape[0] == B else pl.BlockSpec((nb, 2, HEAD_DIM, A_WIDTH), lambda b, c: (0, 0, 0, 0))
    assert s0.shape[0] in (B, nb)
    return pl.pallas_call(
        _wkv_kernel,
        grid=(B // nb, nc),
        in_specs=[fwd] * 6 + [bwd] * 6 + [st_in],
        out_specs=[fwd, bwd, st],
        out_shape=[jax.ShapeDtypeStruct((B, T, A_WIDTH), BF16), jax.ShapeDtypeStruct((B, T, A_WIDTH), BF16),
                   jax.ShapeDtypeStruct((B, 2, HEAD_DIM, A_WIDTH), F32)],
        scratch_shapes=[pltpu.VMEM((nb, 2, HEAD_DIM, A_WIDTH), F32)],
        compiler_params=pltpu.CompilerParams(dimension_semantics=("parallel", "arbitrary"),
                                             vmem_limit_bytes=VMEM_LIMIT),
        name="wkv",
    )(r, kk, v, lw0, kd0, b0, r, kk, v, lw1, kd1, b1, s0)


def _post_kernel(*refs, latent, seq_len):
    if latent:
        (x_ref, mod_ref, yf_ref, yb_ref, bv_ref, ga_ref, cu_ref, cup_ref, cun_ref, bg_ref,
         gnw_ref, gnb_ref, cw_ref, wout_ref, fg_ref, ones_ref, o_ref) = refs
    else:
        (x_ref, mod_ref, yf_ref, yb_ref, bv_ref, ga_ref, cu_ref, bg_ref,
         gnw_ref, gnb_ref, cw_ref, wout_ref, fg_ref, ones_ref, o_ref) = refs
    x = x_ref[0]
    gate = mod_ref[0][:, 2 * D_MODEL:]
    y = yf_ref[0].astype(F32) + yb_ref[0].astype(F32)
    inv_n = 1.0 / HEAD_DIM
    mean = _segsum(y, ones_ref) * inv_n
    dlt = y - mean
    var = _segsum(dlt * dlt, ones_ref) * inv_n
    yn = dlt * lax.rsqrt(var + GN_EPS) * gnw_ref[...] + gnb_ref[...] + bv_ref[0].astype(F32)
    ya = yn * ga_ref[0].astype(F32)

    cu = cu_ref[0].astype(F32)
    w_prev, w_mid, w_next = cw_ref[0:1, :], cw_ref[1:2, :], cw_ref[2:3, :]
    if latent:
        half = B_WIDTH // 2
        t = pl.program_id(1)
        ph, nh = _shift_rows(cu[:, :half], GRID_W)
        top = jnp.where(t == 0, 0.0, cup_ref[0][:, half:].astype(F32))
        bot = jnp.where(t == pl.num_programs(1) - 1, 0.0, cun_ref[0][:, half:].astype(F32))
        pv = jnp.concatenate([top, cu[:cu.shape[0] - HALO, half:]], axis=0)
        nv = jnp.concatenate([cu[HALO:, half:], bot], axis=0)
        prev = jnp.concatenate([ph, pv], axis=1)
        nxt = jnp.concatenate([nh, nv], axis=1)
    else:
        prev, nxt = _shift_rows(cu, seq_len)
    yb = bg_ref[0].astype(F32) * (w_prev * prev + w_mid * cu + w_next * nxt)

    u = _dg(ya.astype(BF16), wout_ref[:A_WIDTH, :]) + _dg(yb.astype(BF16), wout_ref[A_WIDTH:, :])
    o = x + gate * u
    o_ref[0] = o * lax.rsqrt(jnp.mean(o * o, axis=-1, keepdims=True) + NORM_EPS) * fg_ref[...]


def _post(x, mod, yf, yb, bv, ga, cu, bg, weights, latent, seq_len):
    B, T, _ = x.shape
    tile = min(POST_TILE, T)
    nt = T // tile
    tok = lambda w: pl.BlockSpec((1, tile, w), lambda b, t: (b, t, 0))
    mod_spec = pl.BlockSpec((1, 1, 3 * D_MODEL), (lambda b, t: (b, 0, 0)) if mod.shape[0] > 1 else (lambda b, t: (0, 0, 0)))
    per_tile = tile // HALO
    n_halo = T // HALO
    halo_prev = pl.BlockSpec((1, HALO, B_WIDTH), lambda b, t: (b, jnp.maximum(t * per_tile - 1, 0), 0))
    halo_next = pl.BlockSpec((1, HALO, B_WIDTH), lambda b, t: (b, jnp.minimum((t + 1) * per_tile, n_halo - 1), 0))
    acts = [x, mod, yf, yb, bv, ga, cu] + ([cu, cu] if latent else []) + [bg]
    specs = [tok(D_MODEL), mod_spec] + [tok(A_WIDTH)] * 5 + ([halo_prev, halo_next] if latent else []) + [tok(B_WIDTH)]
    return pl.pallas_call(
        functools.partial(_post_kernel, latent=latent, seq_len=seq_len),
        grid=(B, nt),
        in_specs=specs + [_const_spec(w.shape) for w in weights],
        out_specs=tok(D_MODEL),
        out_shape=jax.ShapeDtypeStruct((B, T, D_MODEL), F32),
        compiler_params=pltpu.CompilerParams(dimension_semantics=("parallel", "parallel"),
                                             vmem_limit_bytes=VMEM_LIMIT),
        name="post",
    )(*acts, *weights)


def _block_diag2(w):
    z = jnp.zeros_like(w[0])
    return jnp.concatenate([jnp.concatenate([w[0], z], axis=1), jnp.concatenate([z, w[1]], axis=1)], axis=0)


def kernel(x_prompt, x_sample, c, state_wkv, c_ctx, w_ada, b_ada, norm_g, w_in, shift_mu, decay_w0, decay_down,
           decay_up, iclr_bias, iclr_down, iclr_up, kk_scale, ka_scale, bonus_rk, gn_w, gn_b, conv_w, w_out, final_g):
    assert w_in.shape[0] == 1, "single layer"
    n_lat = c.shape[0]
    row = lambda z: z.reshape(1, -1)

    pad = (-(n_lat + 1)) % 8
    cvec = jnp.concatenate([c, c_ctx[None, :], jnp.zeros((pad, D_MODEL), F32)], axis=0)
    mod = _adaln(cvec, w_ada[0], row(b_ada[0]))
    mod_lat = mod[:n_lat, None, :]
    mod_ctx = mod[n_lat:n_lat + 1, None, :]

    seg_id = np.arange(GROUP) // HEAD_DIM
    ones_bd = jnp.asarray((seg_id[:, None] == seg_id[None, :]).astype(np.float32), dtype=BF16)
    w_down = jnp.concatenate([decay_down[0, 0], decay_down[0, 1], iclr_down[0, 0], iclr_down[0, 1]], axis=1)
    pre_w = [row(norm_g[0]), w_in[0].astype(BF16), w_down.astype(BF16), _block_diag2(decay_up[0]).astype(BF16),
             _block_diag2(iclr_up[0]).astype(BF16), row(decay_w0[0]), row(iclr_bias[0]), shift_mu[0],
             row(kk_scale[0]), row(ka_scale[0]), row(bonus_rk[0]), ones_bd]
    post_w = [row(gn_w[0]), row(gn_b[0]), conv_w[0], w_out[0].astype(BF16), row(final_g), ones_bd]

    def to_state(s):
        return jnp.transpose(s, (0, 1, 4, 2, 3)).reshape(s.shape[0], 2, HEAD_DIM, A_WIDTH)

    def from_state(s):
        return jnp.transpose(s.reshape(s.shape[0], 2, HEAD_DIM, A_HEADS, HEAD_DIM), (0, 1, 3, 4, 2))

    def layer(x, mod_x, s0, latent):
        nb, t, _ = x.shape
        seg = GRID_W if latent else t
        fold = 1 if latent else math.gcd(nb, max(1, TOKEN_TILE // t))
        fold_in = lambda z: z.reshape(nb // fold, fold * t, z.shape[-1])
        outs = _pre(fold_in(x), mod_x, pre_w, seg)
        r, v, kk, lw0, lw1, kd0, kd1, b0, b1 = (z.reshape(nb, t, A_WIDTH) for z in outs[:9])
        bv, ga, cu, bg = outs[9:]
        yf, yb, st = _wkv(r, kk, v, lw0, lw1, kd0, kd1, b0, b1, s0)
        out = _post(fold_in(x), mod_x, fold_in(yf), fold_in(yb), bv, ga, cu, bg, post_w, latent, seg)
        return out.reshape(nb, t, D_MODEL), st

    assert x_prompt.shape[1] <= TOKEN_TILE, "a context sequence must fit one token tile (its conv has no halo)"
    s_zero = jnp.zeros((WKV_BATCH, 2, HEAD_DIM, A_WIDTH), F32)
    y_prompt, s_ctx = layer(x_prompt, mod_ctx, s_zero, False)
    y_sample, _ = layer(x_sample, mod_lat, to_state(state_wkv[:, 0]), True)
    new_state = from_state(s_ctx)[:, None]
    return (y_prompt, y_sample, new_state)
```
